```python
import jax
import jax.numpy as jnp
from jax import lax
import numpy as np

D_MODEL = 1024
BATCH = 2
SEQ = 8192
DEPTH = 1

CHUNK = 64
D_MIX = D_MODEL
D_CONV = D_MIX // 2
D_LRU = D_MIX - D_CONV
CONV_GROUPS = 8
LRU_HEADS = 8
LRU_HEAD_DIM = D_LRU // LRU_HEADS
SHORT_CONV_WIDTH = 3
LRU_CONV_WIDTH = 4
LRU_C = 8.0
IN_COLS = 3 * D_CONV + 2 * D_LRU
N_EXPERTS = 32
TOP_K = 4
D_EXPERT = D_MODEL
SWIGLU_LIMIT = 7.0
SWIGLU_ALPHA = 1.702
EXPERT_BLOCK = 256
LN_EPS = 1e-5
DEEPNORM_ALPHA = (2.0 * DEPTH) ** 0.25
DEEPNORM_BETA = (8.0 * DEPTH) ** -0.25

kernel_name = "hybrid_conv_rglru_moe_deepnorm"


def _layer_norm(x, g, b):
    xf = x.astype(jnp.float32)
    mu = jnp.mean(xf, axis=-1, keepdims=True)
    var = jnp.mean(jnp.square(xf - mu), axis=-1, keepdims=True)
    return ((xf - mu) * lax.rsqrt(var + LN_EPS) * g.astype(jnp.float32) + b.astype(jnp.float32)).astype(x.dtype)


def _causal_depthwise_conv(u, w):
    k = w.shape[0]
    return lax.conv_general_dilated(
        u, w[:, None, :].astype(u.dtype), window_strides=(1,), padding=[(k - 1, 0)],
        dimension_numbers=('NWC', 'WIO', 'NWC'), feature_group_count=u.shape[-1])


def _linear_recurrence_combine(left, right):
    a_l, b_l = left
    a_r, b_r = right
    return a_l * a_r, a_r * b_l + b_r


def _rg_lru(u, w_r, b_r, w_i, b_i, lam):
    bsz, seq, width = u.shape
    uh = u.reshape(bsz, seq, LRU_HEADS, LRU_HEAD_DIM)
    r = jax.nn.sigmoid((jnp.einsum('bshi,hij->bshj', uh, w_r).reshape(bsz, seq, width) + b_r).astype(jnp.float32))
    i = jax.nn.sigmoid((jnp.einsum('bshi,hij->bshj', uh, w_i).reshape(bsz, seq, width) + b_i).astype(jnp.float32))
    log_a = -LRU_C * r * jax.nn.softplus(-lam.astype(jnp.float32))
    a = jnp.exp(log_a)
    b = jnp.sqrt(-jnp.expm1(2.0 * log_a)) * (i * u.astype(jnp.float32))
    _, h = lax.associative_scan(_linear_recurrence_combine, (a, b), axis=1)
    return h.astype(u.dtype)


def _mixer(x, w_in, conv_w, lru_conv_w, lru_conv_b, w_rgate, b_rgate, w_igate, b_igate, lru_lambda, w_out):
    proj = jnp.einsum('bsd,dc->bsc', x, w_in)
    c_b, c_c, c_h, r_x, r_g = jnp.split(
        proj, [D_CONV, 2 * D_CONV, 3 * D_CONV, 3 * D_CONV + D_LRU], axis=-1)
    y_conv = c_b * _causal_depthwise_conv(c_c * c_h, conv_w)
    u = _causal_depthwise_conv(r_x, lru_conv_w) + lru_conv_b
    y_lru = _rg_lru(u, w_rgate, b_rgate, w_igate, b_igate, lru_lambda) * jax.nn.gelu(r_g)
    mix = jnp.concatenate([y_conv, y_lru], axis=-1)
    return jnp.einsum('bsc,cd->bsd', mix, w_out)


def _moe(x, w_router, b_router, w_gu, b_gu, w_dn, b_dn):
    bsz, seq, d = x.shape
    n_tok = bsz * seq
    n_assign = n_tok * TOP_K
    n_blocks = -(-n_assign // EXPERT_BLOCK) + N_EXPERTS
    xt = x.reshape(n_tok, d)
    logits = (xt @ w_router + b_router).astype(jnp.float32)
    top_v, top_e = lax.top_k(logits, TOP_K)
    gates = jax.nn.softmax(top_v, axis=-1).astype(x.dtype)
    flat_e = top_e.reshape(-1)
    order = jnp.argsort(flat_e)
    e_sorted = flat_e[order]
    counts = jnp.bincount(flat_e, length=N_EXPERTS)
    padded = (counts + EXPERT_BLOCK - 1) // EXPERT_BLOCK * EXPERT_BLOCK
    pad_end = jnp.cumsum(padded)
    pad_start = pad_end - padded
    grp_start = jnp.cumsum(counts) - counts
    dest = pad_start[e_sorted] + jnp.arange(n_assign) - grp_start[e_sorted]
    x_pad = jnp.zeros((n_blocks * EXPERT_BLOCK, d), x.dtype).at[dest].set(xt[order // TOP_K])
    block_e = jnp.minimum(
        jnp.searchsorted(pad_end, jnp.arange(n_blocks) * EXPERT_BLOCK, side='right'), N_EXPERTS - 1)

    def expert_block(args):
        xb, e = args
        h = xb @ w_gu[e] + b_gu[e]
        h_glu = jnp.minimum(h[:, :D_EXPERT], SWIGLU_LIMIT)
        h_lin = jnp.clip(h[:, D_EXPERT:], -SWIGLU_LIMIT, SWIGLU_LIMIT)
        act = h_glu * jax.nn.sigmoid(SWIGLU_ALPHA * h_glu) * (h_lin + 1.0)
        return act @ w_dn[e] + b_dn[e]

    y_pad = lax.map(expert_block, (x_pad.reshape(n_blocks, EXPERT_BLOCK, d), block_e))
    y_pad = y_pad.reshape(n_blocks * EXPERT_BLOCK, d)
    slot = jnp.zeros((n_assign,), dest.dtype).at[order].set(dest)
    y = y_pad[slot].reshape(n_tok, TOP_K, d)
    return jnp.einsum('tkd,tk->td', y, gates).reshape(bsz, seq, d)


def setup_inputs(seed: int = 0) -> dict:
    key = jax.random.key(seed)
    ks = jax.random.split(key, 22)
    f32 = jnp.float32
    nrm = lambda k, shape, s: jax.random.normal(k, shape, f32) * s
    u = jax.random.uniform(ks[11], (DEPTH, D_LRU), f32, minval=0.9, maxval=0.999)
    s = u ** (1.0 / LRU_C)
    lru_lambda = jnp.log(s) - jnp.log1p(-s)
    return {
        "x": nrm(ks[0], (BATCH, SEQ, D_MODEL), 1.0),
        "w_in": nrm(ks[1], (DEPTH, D_MODEL, IN_COLS), D_MODEL ** -0.5),
        "conv_w": nrm(ks[2], (DEPTH, SHORT_CONV_WIDTH, D_CONV), SHORT_CONV_WIDTH ** -0.5),
        "lru_conv_w": nrm(ks[3], (DEPTH, LRU_CONV_WIDTH, D_LRU), LRU_CONV_WIDTH ** -0.5),
        "lru_conv_b": nrm(ks[4], (DEPTH, D_LRU), 0.01),
        "w_rgate": nrm(ks[5], (DEPTH, LRU_HEADS, LRU_HEAD_DIM, LRU_HEAD_DIM), LRU_HEAD_DIM ** -0.5),
        "b_rgate": nrm(ks[6], (DEPTH, D_LRU), 0.01),
        "w_igate": nrm(ks[7], (DEPTH, LRU_HEADS, LRU_HEAD_DIM, LRU_HEAD_DIM), LRU_HEAD_DIM ** -0.5),
        "b_igate": nrm(ks[8], (DEPTH, D_LRU), 0.01),
        "lru_lambda": lru_lambda,
        "w_out": nrm(ks[9], (DEPTH, D_MIX, D_MODEL), D_MIX ** -0.5 * DEEPNORM_BETA),
        "ln1_g": 1.0 + nrm(ks[10], (DEPTH, D_MODEL), 0.02),
        "ln1_b": nrm(ks[12], (DEPTH, D_MODEL), 0.02),
        "w_router": nrm(ks[13], (DEPTH, D_MODEL, N_EXPERTS), D_MODEL ** -0.5),
        "b_router": nrm(ks[14], (DEPTH, N_EXPERTS), 0.01),
        "w_gate_up": nrm(ks[15], (DEPTH, N_EXPERTS, D_MODEL, 2 * D_EXPERT), D_MODEL ** -0.5),
        "b_gate_up": nrm(ks[16], (DEPTH, N_EXPERTS, 2 * D_EXPERT), 0.01),
        "w_down": nrm(ks[17], (DEPTH, N_EXPERTS, D_EXPERT, D_MODEL), D_EXPERT ** -0.5 * DEEPNORM_BETA),
        "b_down": nrm(ks[18], (DEPTH, N_EXPERTS, D_MODEL), 0.01),
        "ln2_g": 1.0 + nrm(ks[19], (DEPTH, D_MODEL), 0.02),
        "ln2_b": nrm(ks[20], (DEPTH, D_MODEL), 0.02),
    }


def reference(x, w_in, conv_w, lru_conv_w, lru_conv_b, w_rgate, b_rgate, w_igate, b_igate, lru_lambda,
              w_out, ln1_g, ln1_b, w_router, b_router, w_gate_up, b_gate_up, w_down, b_down, ln2_g, ln2_b):
    for l in range(DEPTH):
        mixed = _mixer(x, w_in[l], conv_w[l], lru_conv_w[l], lru_conv_b[l], w_rgate[l], b_rgate[l],
                       w_igate[l], b_igate[l], lru_lambda[l], w_out[l])
        x = _layer_norm(DEEPNORM_ALPHA * x + mixed, ln1_g[l], ln1_b[l])
        ffn = _moe(x, w_router[l], b_router[l], w_gate_up[l], b_gate_up[l], w_down[l], b_down[l])
        x = _layer_norm(DEEPNORM_ALPHA * x + ffn, ln2_g[l], ln2_b[l])
    return x
```

```python
import functools

import jax
import jax.numpy as jnp
from jax import lax
from jax.experimental import pallas as pl
from jax.experimental.pallas import tpu as pltpu

SHORT_CONV_WIDTH = 3
LRU_CONV_WIDTH = 4
LRU_C = 8.0
N_EXPERTS = 32
TOP_K = 4
SWIGLU_LIMIT = 7.0
SWIGLU_ALPHA = 1.702
LN_EPS = 1e-5

LANES = 128
SUBLANES = 8
MXU_DIM = 256
VMEM_LIMIT_BYTES = 56 * 1024 * 1024

SEQ_TILE = 512
TOKEN_TILE = 256
ROW_BLOCK = 256
GATE_GROUP = MXU_DIM


def _layer_norm(z, g, b):
    mu = jnp.mean(z, axis=-1, keepdims=True)
    zc = z - mu
    var = jnp.mean(zc * zc, axis=-1, keepdims=True)
    return zc * lax.rsqrt(var + LN_EPS) * g + b


def _load_rows(ref, base, n):
    return jnp.concatenate(
        [ref[pl.ds(base * SUBLANES + s, n, stride=SUBLANES), :] for s in range(SUBLANES)], axis=-1)


def _store_rows(ref, val):
    n = val.shape[0]
    for s in range(SUBLANES):
        ref[pl.ds(s, n, stride=SUBLANES), :] = val[:, s * LANES:(s + 1) * LANES]


def _row_tile(ref, row8):
    return ref.at[pl.ds(pl.multiple_of(row8, SUBLANES), SUBLANES), :]


def _mixer_kernel(alpha, d_conv, d_lru,
                  x_ref, w_in_ref, conv_w_ref, lconv_w_ref, lconv_b_ref, w_gate_ref,
                  b_r_ref, b_i_ref, lam_ref, w_out_ref, ln_g_ref, ln_b_ref, w_rt_ref, b_rt_ref,
                  x1t_ref, route_ref, gates_ref, counts_ref,
                  g_buf, rx_buf, h_carry, cnt_carry, tri_buf):
    ts = x_ref.shape[0]
    hdr = SUBLANES
    first_of_seq = pl.program_id(1) == 0
    first_step = jnp.logical_and(pl.program_id(0) == 0, first_of_seq)

    @pl.when(first_of_seq)
    def _():
        g_buf[0:hdr, :] = jnp.zeros((hdr, d_conv), jnp.float32)
        rx_buf[0:hdr, :] = jnp.zeros((hdr, d_lru), jnp.float32)
        h_carry[...] = jnp.zeros_like(h_carry)

    @pl.when(first_step)
    def _():
        cnt_carry[...] = jnp.zeros_like(cnt_carry)
        r = lax.broadcasted_iota(jnp.int32, (ts, ts), 0)
        c = lax.broadcasted_iota(jnp.int32, (ts, ts), 1)
        tri_buf[...] = (c < r).astype(jnp.bfloat16)

    x = x_ref[...]
    xb = x.astype(jnp.bfloat16)

    def proj(lo, hi):
        return jnp.dot(xb, w_in_ref[:, lo:hi], preferred_element_type=jnp.float32)

    c_b = proj(0, d_conv)
    g = proj(d_conv, 2 * d_conv) * proj(2 * d_conv, 3 * d_conv)
    g_buf[hdr:hdr + ts, :] = g
    y_conv = conv_w_ref[SHORT_CONV_WIDTH - 1:SHORT_CONV_WIDTH, :] * g
    for k in range(SHORT_CONV_WIDTH - 1):
        back = SHORT_CONV_WIDTH - 1 - k
        y_conv = y_conv + conv_w_ref[k:k + 1, :] * g_buf[hdr - back:hdr - back + ts, :]
    y_conv = c_b * y_conv
    g_buf[0:hdr, :] = g_buf[ts:ts + hdr, :]

    off = 3 * d_conv
    r_x = proj(off, off + d_lru)
    rx_buf[hdr:hdr + ts, :] = r_x
    u = lconv_w_ref[LRU_CONV_WIDTH - 1:LRU_CONV_WIDTH, :] * r_x + lconv_b_ref[...]
    for k in range(LRU_CONV_WIDTH - 1):
        back = LRU_CONV_WIDTH - 1 - k
        u = u + lconv_w_ref[k:k + 1, :] * rx_buf[hdr - back:hdr - back + ts, :]
    rx_buf[0:hdr, :] = rx_buf[ts:ts + hdr, :]

    ub = u.astype(jnp.bfloat16)
    r_parts, i_parts = [], []
    for gi in range(d_lru // GATE_GROUP):
        ri = jnp.dot(ub[:, gi * GATE_GROUP:(gi + 1) * GATE_GROUP], w_gate_ref[gi],
                     preferred_element_type=jnp.float32)
        r_parts.append(ri[:, :GATE_GROUP])
        i_parts.append(ri[:, GATE_GROUP:])
    r = jax.nn.sigmoid(jnp.concatenate(r_parts, axis=-1) + b_r_ref[...])
    i = jax.nn.sigmoid(jnp.concatenate(i_parts, axis=-1) + b_i_ref[...])

    neg_lam = -lam_ref[...]
    softplus = jnp.maximum(neg_lam, 0.0) + jnp.log1p(jnp.exp(-jnp.abs(neg_lam)))
    log_a = (-LRU_C) * r * softplus
    a = jnp.exp(log_a)
    bb = jnp.sqrt(jnp.tanh(-log_a) * (a * a + 1.0)) * (i * u)

    row = lax.broadcasted_iota(jnp.int32, (ts, d_lru), 0)
    d = 1
    while d < ts:
        keep = row >= d
        a_sh = jnp.where(keep, pltpu.roll(a, d, 0), 1.0)
        b_sh = jnp.where(keep, pltpu.roll(bb, d, 0), 0.0)
        bb = a * b_sh + bb
        a = a * a_sh
        d *= 2
    h = a * h_carry[...] + bb
    h_carry[...] = h[ts - 1:ts, :]

    r_g = proj(off + d_lru, off + 2 * d_lru)
    y_lru = h * jax.nn.gelu(r_g)

    mixed = (jnp.dot(y_conv.astype(jnp.bfloat16), w_out_ref[0:d_conv, :],
                     preferred_element_type=jnp.float32)
             + jnp.dot(y_lru.astype(jnp.bfloat16), w_out_ref[d_conv:d_conv + d_lru, :],
                       preferred_element_type=jnp.float32))
    x1 = _layer_norm(alpha * x + mixed, ln_g_ref[...], ln_b_ref[...])
    _store_rows(x1t_ref, x1)

    logits = jnp.dot(x1, w_rt_ref[...], preferred_element_type=jnp.float32,
                     precision=lax.Precision.HIGHEST) + b_rt_ref[...]
    lane = lax.broadcasted_iota(jnp.int32, (ts, LANES), 1)
    lane_f = lane.astype(jnp.float32)
    neg_inf = jnp.float32(-jnp.inf)
    work = jnp.where(lane < N_EXPERTS, logits, neg_inf)
    sel = jnp.zeros((ts, LANES), jnp.bool_)
    top_v, top_e = [], []
    for _ in range(TOP_K):
        m = jnp.max(work, axis=-1, keepdims=True)
        e_f = jnp.min(jnp.where(work == m, lane_f, float(LANES)), axis=-1, keepdims=True)
        hit = lane_f == e_f
        sel = jnp.logical_or(sel, hit)
        work = jnp.where(hit, neg_inf, work)
        top_v.append(m)
        top_e.append(e_f.astype(jnp.int32))
    exps = [jnp.exp(v - top_v[0]) for v in top_v]
    denom = exps[0]
    for ex in exps[1:]:
        denom = denom + ex

    sel_f = sel.astype(jnp.float32)
    before = jnp.dot(tri_buf[...], sel_f.astype(jnp.bfloat16), preferred_element_type=jnp.float32)
    pos = before + cnt_carry[...]
    cnt_carry[...] = cnt_carry[...] + jnp.sum(sel_f, axis=0, keepdims=True)
    counts_ref[...] = cnt_carry[...].astype(jnp.int32)

    route = jnp.zeros((ts, LANES), jnp.int32)
    gates = jnp.zeros((ts, LANES), jnp.float32)
    for k in range(TOP_K):
        rank = jnp.sum(jnp.where(lane == top_e[k], pos, 0.0), axis=-1, keepdims=True)
        route = jnp.where(lane == k, top_e[k], route)
        route = jnp.where(lane == TOP_K + k, rank.astype(jnp.int32), route)
        gates = jnp.where(lane == k, exps[k] / denom, gates)
    route_ref[...] = route
    gates_ref[...] = gates


def _mixer_call(x, w_in, conv_w, lconv_w, lconv_b, w_gate, b_r, b_i, lam, w_out, ln_g, ln_b,
                w_rt, b_rt, alpha):
    bsz, seq, d = x.shape
    d_conv = conv_w.shape[1]
    d_lru = lconv_w.shape[1]
    ts = SEQ_TILE
    n_tok = bsz * seq
    nt = seq // ts

    def const(shape):
        return pl.BlockSpec(shape, lambda b, t: (0,) * len(shape))

    tok_map = lambda b, t: (b * nt + t, 0)
    kernel = functools.partial(_mixer_kernel, alpha, d_conv, d_lru)
    return pl.pallas_call(
        kernel,
        grid=(bsz, nt),
        in_specs=[
            pl.BlockSpec((None, ts, d), lambda b, t: (b, t, 0)),
            const(w_in.shape), const(conv_w.shape), const(lconv_w.shape), const(lconv_b.shape),
            const(w_gate.shape), const(b_r.shape), const(b_i.shape), const(lam.shape),
            const(w_out.shape), const(ln_g.shape), const(ln_b.shape), const(w_rt.shape),
            const(b_rt.shape),
        ],
        out_specs=[
            pl.BlockSpec((ts * SUBLANES, LANES), tok_map),
            pl.BlockSpec((ts, LANES), tok_map),
            pl.BlockSpec((ts, LANES), tok_map),
            pl.BlockSpec((1, LANES), lambda b, t: (0, 0)),
        ],
        out_shape=[
            jax.ShapeDtypeStruct((n_tok * SUBLANES, LANES), jnp.float32),
            jax.ShapeDtypeStruct((n_tok, LANES), jnp.int32),
            jax.ShapeDtypeStruct((n_tok, LANES), jnp.float32),
            jax.ShapeDtypeStruct((1, LANES), jnp.int32),
        ],
        scratch_shapes=[
            pltpu.VMEM((ts + SUBLANES, d_conv), jnp.float32),
            pltpu.VMEM((ts + SUBLANES, d_lru), jnp.float32),
            pltpu.VMEM((1, d_lru), jnp.float32),
            pltpu.VMEM((1, LANES), jnp.float32),
            pltpu.VMEM((ts, ts), jnp.bfloat16),
        ],
        compiler_params=pltpu.CompilerParams(
            dimension_semantics=("arbitrary", "arbitrary"),
            vmem_limit_bytes=VMEM_LIMIT_BYTES),
        name="mixer_ln1_router",
    )(x, w_in, conv_w, lconv_w, lconv_b, w_gate, b_r, b_i, lam, w_out, ln_g, ln_b, w_rt, b_rt)


def _dispatch_kernel(slot8_ref, zpos8_ref, x_ref, xs_ref, zero_buf, sem, zsem):
    tt = x_ref.shape[0] // SUBLANES
    blk8 = zero_buf.shape[0]

    @pl.when(pl.program_id(0) == 0)
    def _():
        zero_buf[...] = jnp.zeros_like(zero_buf)

        def zcopy(e):
            start = pl.multiple_of(zpos8_ref[e], SUBLANES)
            return pltpu.make_async_copy(zero_buf, xs_ref.at[pl.ds(start, blk8), :], zsem)

        def zstart(e, c):
            zcopy(e).start()
            return c

        def zwait(e, c):
            zcopy(e).wait()
            return c

        lax.fori_loop(0, N_EXPERTS, zstart, 0)
        lax.fori_loop(0, N_EXPERTS, zwait, 0)

        def tcopy(b):
            return pltpu.make_async_copy(zero_buf, xs_ref.at[pl.ds(pl.multiple_of(b * blk8, blk8), blk8), :],
                                         zsem)

        def tstart(b, c):
            tcopy(b).start()
            return c

        def twait(b, c):
            tcopy(b).wait()
            return c

        n_used = zpos8_ref[N_EXPERTS]
        n_total = xs_ref.shape[0] // blk8
        lax.fori_loop(n_used, n_total, tstart, 0)
        lax.fori_loop(n_used, n_total, twait, 0)

    def row_copy(j, k):
        return pltpu.make_async_copy(_row_tile(x_ref, j * SUBLANES),
                                     _row_tile(xs_ref, slot8_ref[j * TOP_K + k]), sem)

    def start(j, c):
        for k in range(TOP_K):
            row_copy(j, k).start()
        return c

    def wait(j, c):
        for k in range(TOP_K):
            row_copy(j, k).wait()
        return c

    lax.fori_loop(0, tt, start, 0)
    lax.fori_loop(0, tt, wait, 0)


def _dispatch_call(slot8, zpos8, x1t, n_rows):
    n_tok = x1t.shape[0] // SUBLANES
    tt = TOKEN_TILE
    return pl.pallas_call(
        _dispatch_kernel,
        grid=(n_tok // tt,),
        in_specs=[
            pl.BlockSpec((tt * TOP_K,), lambda i: (i,), memory_space=pltpu.SMEM),
            pl.BlockSpec(memory_space=pltpu.SMEM),
            pl.BlockSpec((tt * SUBLANES, LANES), lambda i: (i, 0)),
        ],
        out_specs=pl.BlockSpec(memory_space=pl.ANY),
        out_shape=jax.ShapeDtypeStruct(((n_rows + ROW_BLOCK) * SUBLANES, LANES), jnp.float32),
        scratch_shapes=[
            pltpu.VMEM((ROW_BLOCK * SUBLANES, LANES), jnp.float32),
            pltpu.SemaphoreType.DMA,
            pltpu.SemaphoreType.DMA,
        ],
        compiler_params=pltpu.CompilerParams(
            dimension_semantics=("arbitrary",),
            vmem_limit_bytes=VMEM_LIMIT_BYTES),
        name="dispatch_rows",
    )(slot8, zpos8, x1t)


def _expert_kernel(d_expert, block_e_ref, n_used_ref, xs_ref, w_gu_ref, b_gu_ref, w_dn_ref, b_dn_ref,
                   ys_ref):
    bm = xs_ref.shape[0] // SUBLANES

    @pl.when(pl.program_id(0) < n_used_ref[0])
    def _():
        xb = _load_rows(xs_ref, 0, bm)
        h = jnp.dot(xb, w_gu_ref[...], preferred_element_type=jnp.float32) + b_gu_ref[...]
        h_glu = jnp.minimum(h[:, :d_expert], SWIGLU_LIMIT)
        h_lin = jnp.clip(h[:, d_expert:], -SWIGLU_LIMIT, SWIGLU_LIMIT)
        act = h_glu * jax.nn.sigmoid(SWIGLU_ALPHA * h_glu) * (h_lin + 1.0)
        y = jnp.dot(act, w_dn_ref[...], preferred_element_type=jnp.float32) + b_dn_ref[...]
        _store_rows(ys_ref, y)


def _expert_call(block_e, n_used, xs, w_gu, b_gu, w_dn, b_dn, n_blocks):
    d = w_gu.shape[1]
    d_expert = w_dn.shape[1]
    bm = ROW_BLOCK

    def row_map(m, be, nu):
        return (jnp.minimum(m, nu[0] - 1), 0)

    def exp_map3(m, be, nu):
        return (be[jnp.minimum(m, nu[0] - 1)], 0, 0)

    grid_spec = pltpu.PrefetchScalarGridSpec(
        num_scalar_prefetch=2,
        grid=(n_blocks,),
        in_specs=[
            pl.BlockSpec((bm * SUBLANES, LANES), row_map),
            pl.BlockSpec((None, d, 2 * d_expert), exp_map3),
            pl.BlockSpec((None, 1, 2 * d_expert), exp_map3),
            pl.BlockSpec((None, d_expert, d), exp_map3),
            pl.BlockSpec((None, 1, d), exp_map3),
        ],
        out_specs=pl.BlockSpec((bm * SUBLANES, LANES), row_map),
    )
    return pl.pallas_call(
        functools.partial(_expert_kernel, d_expert),
        grid_spec=grid_spec,
        out_shape=jax.ShapeDtypeStruct((n_blocks * bm * SUBLANES, LANES), jnp.float32),
        compiler_params=pltpu.CompilerParams(
            dimension_semantics=("arbitrary",),
            vmem_limit_bytes=VMEM_LIMIT_BYTES),
        name="expert_ffn",
    )(block_e, n_used, xs, w_gu, b_gu, w_dn, b_dn)


def _combine_kernel(alpha, slot8_ref, ys_ref, x1t_ref, gates_ref, ln_g_ref, ln_b_ref, out_ref,
                    y_buf, sem):
    tt = out_ref.shape[0]

    def row_copy(j, k):
        return pltpu.make_async_copy(_row_tile(ys_ref, slot8_ref[j * TOP_K + k]),
                                     _row_tile(y_buf, (k * tt + j) * SUBLANES), sem)

    def start(j, c):
        for k in range(TOP_K):
            row_copy(j, k).start()
        return c

    def wait(j, c):
        for k in range(TOP_K):
            row_copy(j, k).wait()
        return c

    lax.fori_loop(0, tt, start, 0)
    lax.fori_loop(0, tt, wait, 0)

    gates = gates_ref[...]
    z = alpha * _load_rows(x1t_ref, 0, tt)
    for k in range(TOP_K):
        z = z + gates[:, k:k + 1] * _load_rows(y_buf, k * tt, tt)
    out_ref[...] = _layer_norm(z, ln_g_ref[...], ln_b_ref[...])


def _combine_call(slot8, ys, x1t, gates, ln_g, ln_b, alpha):
    n_tok = gates.shape[0]
    d = ln_g.shape[1]
    tt = TOKEN_TILE
    return pl.pallas_call(
        functools.partial(_combine_kernel, alpha),
        grid=(n_tok // tt,),
        in_specs=[
            pl.BlockSpec((tt * TOP_K,), lambda i: (i,), memory_space=pltpu.SMEM),
            pl.BlockSpec(memory_space=pl.ANY),
            pl.BlockSpec((tt * SUBLANES, LANES), lambda i: (i, 0)),
            pl.BlockSpec((tt, LANES), lambda i: (i, 0)),
            pl.BlockSpec((1, d), lambda i: (0, 0)),
            pl.BlockSpec((1, d), lambda i: (0, 0)),
        ],
        out_specs=pl.BlockSpec((tt, d), lambda i: (i, 0)),
        out_shape=jax.ShapeDtypeStruct((n_tok, d), jnp.float32),
        scratch_shapes=[
            pltpu.VMEM((TOP_K * tt * SUBLANES, LANES), jnp.float32),
            pltpu.SemaphoreType.DMA,
        ],
        compiler_params=pltpu.CompilerParams(
            dimension_semantics=("arbitrary",),
            vmem_limit_bytes=VMEM_LIMIT_BYTES),
        name="combine_ln2",
    )(slot8, ys, x1t, gates, ln_g, ln_b)


def _block_diag_gates(w_r, w_i):
    n_heads, dh, _ = w_r.shape
    per = GATE_GROUP // dh
    groups = []
    for gi in range(n_heads // per):
        def bd(w):
            m = jnp.zeros((GATE_GROUP, GATE_GROUP), w.dtype)
            for j in range(per):
                m = lax.dynamic_update_slice(m, w[gi * per + j], (j * dh, j * dh))
            return m
        groups.append(jnp.concatenate([bd(w_r), bd(w_i)], axis=1))
    return jnp.stack(groups)


def _layer(x, w_in, conv_w, lconv_w, lconv_b, w_r, b_r, w_i, b_i, lam, w_out, ln1_g, ln1_b,
           w_rt, b_rt, w_gu, b_gu, w_dn, b_dn, ln2_g, ln2_b, alpha):
    bsz, seq, d = x.shape
    assert d == SUBLANES * LANES, "tile-per-row layout needs D == 1024"
    n_tok = bsz * seq
    n_assign = n_tok * TOP_K
    n_blocks = -(-n_assign // ROW_BLOCK) + N_EXPERTS
    n_rows = n_blocks * ROW_BLOCK
    bf16 = jnp.bfloat16
    row = lambda v: v.reshape(1, -1)

    w_rt_p = jnp.zeros((d, LANES), jnp.float32).at[:, :N_EXPERTS].set(w_rt)
    b_rt_p = jnp.zeros((1, LANES), jnp.float32).at[0, :N_EXPERTS].set(b_rt)
    x1t, route, gates, counts = _mixer_call(
        x, w_in.astype(bf16), conv_w, lconv_w, row(lconv_b), _block_diag_gates(w_r, w_i).astype(bf16),
        row(b_r), row(b_i), row(lam), w_out.astype(bf16), row(ln1_g), row(ln1_b), w_rt_p, b_rt_p, alpha)

    counts = counts[0, :N_EXPERTS]
    padded = (counts + ROW_BLOCK - 1) // ROW_BLOCK * ROW_BLOCK
    pad_end = jnp.cumsum(padded)
    pad_start = pad_end - padded
    top_e = route[:, :TOP_K]
    rank = route[:, TOP_K:2 * TOP_K]
    onehot = top_e[:, :, None] == jnp.arange(N_EXPERTS, dtype=jnp.int32)
    slot = jnp.sum(jnp.where(onehot, pad_start, 0), axis=-1) + rank
    slot8 = (slot.reshape(-1) * SUBLANES).astype(jnp.int32)
    block_e = jnp.minimum(
        jnp.searchsorted(pad_end, jnp.arange(n_blocks, dtype=jnp.int32) * ROW_BLOCK, side='right'),
        N_EXPERTS - 1).astype(jnp.int32)
    n_used = (pad_end[-1:] // ROW_BLOCK).astype(jnp.int32)
    zpos8 = jnp.concatenate([((pad_start + counts) * SUBLANES).astype(jnp.int32), n_used])

    xs = _dispatch_call(slot8, zpos8, x1t, n_rows)
    ys = _expert_call(block_e, n_used, xs, w_gu, b_gu[:, None, :], w_dn, b_dn[:, None, :], n_blocks)
    out = _combine_call(slot8, ys, x1t, gates, row(ln2_g), row(ln2_b), alpha)
    return out.reshape(bsz, seq, d)


def kernel(x, w_in, conv_w, lru_conv_w, lru_conv_b, w_rgate, b_rgate, w_igate, b_igate, lru_lambda,
           w_out, ln1_g, ln1_b, w_router, b_router, w_gate_up, b_gate_up, w_down, b_down, ln2_g, ln2_b):
    depth = w_in.shape[0]
    alpha = (2.0 * depth) ** 0.25
    for l in range(depth):
        x = _layer(x, w_in[l], conv_w[l], lru_conv_w[l], lru_conv_b[l], w_rgate[l], b_rgate[l],
                   w_igate[l], b_igate[l], lru_lambda[l], w_out[l], ln1_g[l], ln1_b[l],
                   w_router[l], b_router[l], w_gate_up[l], b_gate_up[l], w_down[l], b_down[l],
                   ln2_g[l], ln2_b[l], alpha)
    return x
```

```python
import functools

import jax
import jax.numpy as jnp
from jax import lax
from jax.experimental import pallas as pl
from jax.experimental.pallas import tpu as pltpu

SHORT_CONV_WIDTH = 3
LRU_CONV_WIDTH = 4
LRU_C = 8.0
N_EXPERTS = 32
TOP_K = 4
SWIGLU_LIMIT = 7.0
SWIGLU_ALPHA = 1.702
LN_EPS = 1e-5

LANES = 128
SUBLANES = 8
MXU_DIM = 256
VMEM_LIMIT_BYTES = 56 * 1024 * 1024

SEQ_TILE = 512
TOKEN_TILE = 256
ROW_BLOCK = 512
GATE_GROUP = MXU_DIM
ISSUE_UNROLL = 8


def _layer_norm(z, g, b):
    mu = jnp.mean(z, axis=-1, keepdims=True)
    zc = z - mu
    var = jnp.mean(zc * zc, axis=-1, keepdims=True)
    return zc * lax.rsqrt(var + LN_EPS) * g + b


def _load_rows(ref, base, n):
    return jnp.concatenate(
        [ref[pl.ds(base * SUBLANES + s, n, stride=SUBLANES), :] for s in range(SUBLANES)], axis=-1)


def _store_rows(ref, val):
    n = val.shape[0]
    for s in range(SUBLANES):
        ref[pl.ds(s, n, stride=SUBLANES), :] = val[:, s * LANES:(s + 1) * LANES]


def _row_tile(ref, row8):
    return ref.at[pl.ds(pl.multiple_of(row8, SUBLANES), SUBLANES), :]


def _start_rows(row_copy, n_tok):
    def body(g, c):
        for jj in range(ISSUE_UNROLL):
            for k in range(TOP_K):
                row_copy(g * ISSUE_UNROLL + jj, k).start(priority=k % 2)
        return c
    lax.fori_loop(0, n_tok // ISSUE_UNROLL, body, 0)


def _wait_rows(row_copy, n_tok):
    def body(g, c):
        for jj in range(ISSUE_UNROLL):
            for k in range(TOP_K):
                row_copy(g * ISSUE_UNROLL + jj, k).wait()
        return c
    lax.fori_loop(0, n_tok // ISSUE_UNROLL, body, 0)


def _mixer_kernel(alpha, d_conv, d_lru,
                  x_ref, w_in_ref, conv_w_ref, lconv_w_ref, lconv_b_ref, w_gate_ref,
                  b_r_ref, b_i_ref, lam_ref, w_out_ref, ln_g_ref, ln_b_ref, w_rt_ref, b_rt_ref,
                  x1t_ref, route_ref, gates_ref, counts_ref,
                  g_buf, rx_buf, h_carry, cnt_carry, tri_buf):
    ts = x_ref.shape[0]
    hdr = SUBLANES
    first_of_seq = pl.program_id(1) == 0
    first_step = jnp.logical_and(pl.program_id(0) == 0, first_of_seq)

    @pl.when(first_of_seq)
    def _():
        g_buf[0:hdr, :] = jnp.zeros((hdr, d_conv), jnp.float32)
        rx_buf[0:hdr, :] = jnp.zeros((hdr, d_lru), jnp.float32)
        h_carry[...] = jnp.zeros_like(h_carry)

    @pl.when(first_step)
    def _():
        cnt_carry[...] = jnp.zeros_like(cnt_carry)
        r = lax.broadcasted_iota(jnp.int32, (ts, ts), 0)
        c = lax.broadcasted_iota(jnp.int32, (ts, ts), 1)
        tri_buf[...] = (c < r).astype(jnp.bfloat16)

    x = x_ref[...]
    xb = x.astype(jnp.bfloat16)

    def proj(lo, hi):
        return jnp.dot(xb, w_in_ref[:, lo:hi], preferred_element_type=jnp.float32)

    c_b = proj(0, d_conv)
    g = proj(d_conv, 2 * d_conv) * proj(2 * d_conv, 3 * d_conv)
    g_buf[hdr:hdr + ts, :] = g
    y_conv = conv_w_ref[SHORT_CONV_WIDTH - 1:SHORT_CONV_WIDTH, :] * g
    for k in range(SHORT_CONV_WIDTH - 1):
        back = SHORT_CONV_WIDTH - 1 - k
        y_conv = y_conv + conv_w_ref[k:k + 1, :] * g_buf[hdr - back:hdr - back + ts, :]
    y_conv = c_b * y_conv
    g_buf[0:hdr, :] = g_buf[ts:ts + hdr, :]

    off = 3 * d_conv
    r_x = proj(off, off + d_lru)
    rx_buf[hdr:hdr + ts, :] = r_x
    u = lconv_w_ref[LRU_CONV_WIDTH - 1:LRU_CONV_WIDTH, :] * r_x + lconv_b_ref[...]
    for k in range(LRU_CONV_WIDTH - 1):
        back = LRU_CONV_WIDTH - 1 - k
        u = u + lconv_w_ref[k:k + 1, :] * rx_buf[hdr - back:hdr - back + ts, :]
    rx_buf[0:hdr, :] = rx_buf[ts:ts + hdr, :]

    ub = u.astype(jnp.bfloat16)
    r_parts, i_parts = [], []
    for gi in range(d_lru // GATE_GROUP):
        ri = jnp.dot(ub[:, gi * GATE_GROUP:(gi + 1) * GATE_GROUP], w_gate_ref[gi],
                     preferred_element_type=jnp.float32)
        r_parts.append(ri[:, :GATE_GROUP])
        i_parts.append(ri[:, GATE_GROUP:])
    r = jax.nn.sigmoid(jnp.concatenate(r_parts, axis=-1) + b_r_ref[...])
    i = jax.nn.sigmoid(jnp.concatenate(i_parts, axis=-1) + b_i_ref[...])

    neg_lam = -lam_ref[...]
    softplus = jnp.maximum(neg_lam, 0.0) + jnp.log1p(jnp.exp(-jnp.abs(neg_lam)))
    log_a = (-LRU_C) * r * softplus
    a = jnp.exp(log_a)
    bb = jnp.sqrt(jnp.tanh(-log_a) * (a * a + 1.0)) * (i * u)

    row = lax.broadcasted_iota(jnp.int32, (ts, d_lru), 0)
    d = 1
    while d < ts:
        keep = row >= d
        a_sh = jnp.where(keep, pltpu.roll(a, d, 0), 1.0)
        b_sh = jnp.where(keep, pltpu.roll(bb, d, 0), 0.0)
        bb = a * b_sh + bb
        a = a * a_sh
        d *= 2
    h = a * h_carry[...] + bb
    h_carry[...] = h[ts - 1:ts, :]

    r_g = proj(off + d_lru, off + 2 * d_lru)
    y_lru = h * jax.nn.gelu(r_g)

    mixed = (jnp.dot(y_conv.astype(jnp.bfloat16), w_out_ref[0:d_conv, :],
                     preferred_element_type=jnp.float32)
             + jnp.dot(y_lru.astype(jnp.bfloat16), w_out_ref[d_conv:d_conv + d_lru, :],
                       preferred_element_type=jnp.float32))
    x1 = _layer_norm(alpha * x + mixed, ln_g_ref[...], ln_b_ref[...])
    _store_rows(x1t_ref, x1)

    x1_hi = x1.astype(jnp.bfloat16)
    x1_lo = (x1 - x1_hi.astype(jnp.float32)).astype(jnp.bfloat16)
    hi_part = jnp.dot(x1_hi, w_rt_ref[...], preferred_element_type=jnp.float32)
    lo_part = jnp.dot(x1_lo, w_rt_ref[:, 0:LANES], preferred_element_type=jnp.float32)
    logits = hi_part[:, 0:LANES] + hi_part[:, LANES:2 * LANES] + lo_part + b_rt_ref[...]
    lane = lax.broadcasted_iota(jnp.int32, (ts, LANES), 1)
    lane_f = lane.astype(jnp.float32)
    neg_inf = jnp.float32(-jnp.inf)
    work = jnp.where(lane < N_EXPERTS, logits, neg_inf)
    sel = jnp.zeros((ts, LANES), jnp.bool_)
    top_v, top_e = [], []
    for _ in range(TOP_K):
        m = jnp.max(work, axis=-1, keepdims=True)
        e_f = jnp.min(jnp.where(work == m, lane_f, float(LANES)), axis=-1, keepdims=True)
        hit = lane_f == e_f
        sel = jnp.logical_or(sel, hit)
        work = jnp.where(hit, neg_inf, work)
        top_v.append(m)
        top_e.append(e_f.astype(jnp.int32))
    exps = [jnp.exp(v - top_v[0]) for v in top_v]
    denom = exps[0]
    for ex in exps[1:]:
        denom = denom + ex

    sel_f = sel.astype(jnp.float32)
    before = jnp.dot(tri_buf[...], sel_f.astype(jnp.bfloat16), preferred_element_type=jnp.float32)
    pos = before + cnt_carry[...]
    cnt_carry[...] = cnt_carry[...] + jnp.sum(sel_f, axis=0, keepdims=True)
    counts_ref[...] = cnt_carry[...].astype(jnp.int32)

    route = jnp.zeros((ts, LANES), jnp.int32)
    gates = jnp.zeros((ts, LANES), jnp.float32)
    for k in range(TOP_K):
        rank = jnp.sum(jnp.where(lane == top_e[k], pos, 0.0), axis=-1, keepdims=True)
        route = jnp.where(lane == k, top_e[k], route)
        route = jnp.where(lane == TOP_K + k, rank.astype(jnp.int32), route)
        gates = jnp.where(lane == k, exps[k] / denom, gates)
    route_ref[...] = route
    gates_ref[...] = gates


def _mixer_call(x, w_in, conv_w, lconv_w, lconv_b, w_gate, b_r, b_i, lam, w_out, ln_g, ln_b,
                w_rt, b_rt, alpha):
    bsz, seq, d = x.shape
    d_conv = conv_w.shape[1]
    d_lru = lconv_w.shape[1]
    ts = SEQ_TILE
    n_tok = bsz * seq
    nt = seq // ts

    def const(shape):
        return pl.BlockSpec(shape, lambda b, t: (0,) * len(shape))

    tok_map = lambda b, t: (b * nt + t, 0)
    kernel = functools.partial(_mixer_kernel, alpha, d_conv, d_lru)
    return pl.pallas_call(
        kernel,
        grid=(bsz, nt),
        in_specs=[
            pl.BlockSpec((None, ts, d), lambda b, t: (b, t, 0)),
            const(w_in.shape), const(conv_w.shape), const(lconv_w.shape), const(lconv_b.shape),
            const(w_gate.shape), const(b_r.shape), const(b_i.shape), const(lam.shape),
            const(w_out.shape), const(ln_g.shape), const(ln_b.shape), const(w_rt.shape),
            const(b_rt.shape),
        ],
        out_specs=[
            pl.BlockSpec((ts * SUBLANES, LANES), tok_map),
            pl.BlockSpec((ts, LANES), tok_map),
            pl.BlockSpec((ts, LANES), tok_map),
            pl.BlockSpec((1, LANES), lambda b, t: (0, 0)),
        ],
        out_shape=[
            jax.ShapeDtypeStruct((n_tok * SUBLANES, LANES), jnp.float32),
            jax.ShapeDtypeStruct((n_tok, LANES), jnp.int32),
            jax.ShapeDtypeStruct((n_tok, LANES), jnp.float32),
            jax.ShapeDtypeStruct((1, LANES), jnp.int32),
        ],
        scratch_shapes=[
            pltpu.VMEM((ts + SUBLANES, d_conv), jnp.float32),
            pltpu.VMEM((ts + SUBLANES, d_lru), jnp.float32),
            pltpu.VMEM((1, d_lru), jnp.float32),
            pltpu.VMEM((1, LANES), jnp.float32),
            pltpu.VMEM((ts, ts), jnp.bfloat16),
        ],
        compiler_params=pltpu.CompilerParams(
            dimension_semantics=("arbitrary", "arbitrary"),
            vmem_limit_bytes=VMEM_LIMIT_BYTES),
        name="mixer_ln1_router",
    )(x, w_in, conv_w, lconv_w, lconv_b, w_gate, b_r, b_i, lam, w_out, ln_g, ln_b, w_rt, b_rt)


def _dispatch_kernel(slot8_ref, zpos8_ref, x_ref, xs_ref, zero_buf, sem, zsem):
    tt = x_ref.shape[0] // SUBLANES
    blk8 = zero_buf.shape[0]

    @pl.when(pl.program_id(0) == 0)
    def _():
        zero_buf[...] = jnp.zeros_like(zero_buf)

        def zcopy(e):
            start = pl.multiple_of(zpos8_ref[e], SUBLANES)
            return pltpu.make_async_copy(zero_buf, xs_ref.at[pl.ds(start, blk8), :], zsem)

        def zstart(e, c):
            zcopy(e).start()
            return c

        def zwait(e, c):
            zcopy(e).wait()
            return c

        lax.fori_loop(0, N_EXPERTS, zstart, 0)
        lax.fori_loop(0, N_EXPERTS, zwait, 0)

        def tcopy(b):
            return pltpu.make_async_copy(zero_buf, xs_ref.at[pl.ds(pl.multiple_of(b * blk8, blk8), blk8), :],
                                         zsem)

        def tstart(b, c):
            tcopy(b).start()
            return c

        def twait(b, c):
            tcopy(b).wait()
            return c

        n_used = zpos8_ref[N_EXPERTS]
        n_total = xs_ref.shape[0] // blk8
        lax.fori_loop(n_used, n_total, tstart, 0)
        lax.fori_loop(n_used, n_total, twait, 0)

    def row_copy(j, k):
        return pltpu.make_async_copy(_row_tile(x_ref, j * SUBLANES),
                                     _row_tile(xs_ref, slot8_ref[j * TOP_K + k]), sem)

    _start_rows(row_copy, tt)
    _wait_rows(row_copy, tt)


def _dispatch_call(slot8, zpos8, x1t, n_rows):
    n_tok = x1t.shape[0] // SUBLANES
    tt = TOKEN_TILE
    return pl.pallas_call(
        _dispatch_kernel,
        grid=(n_tok // tt,),
        in_specs=[
            pl.BlockSpec((tt * TOP_K,), lambda i: (i,), memory_space=pltpu.SMEM),
            pl.BlockSpec(memory_space=pltpu.SMEM),
            pl.BlockSpec((tt * SUBLANES, LANES), lambda i: (i, 0)),
        ],
        out_specs=pl.BlockSpec(memory_space=pl.ANY),
        out_shape=jax.ShapeDtypeStruct(((n_rows + ROW_BLOCK) * SUBLANES, LANES), jnp.float32),
        scratch_shapes=[
            pltpu.VMEM((ROW_BLOCK * SUBLANES, LANES), jnp.float32),
            pltpu.SemaphoreType.DMA,
            pltpu.SemaphoreType.DMA,
        ],
        compiler_params=pltpu.CompilerParams(
            dimension_semantics=("arbitrary",),
            vmem_limit_bytes=VMEM_LIMIT_BYTES),
        name="dispatch_rows",
    )(slot8, zpos8, x1t)


def _expert_kernel(d_expert, block_e_ref, n_used_ref, xs_ref, w_gu_ref, b_gu_ref, w_dn_ref, b_dn_ref,
                   ys_ref):
    bm = xs_ref.shape[0] // SUBLANES

    @pl.when(pl.program_id(0) < n_used_ref[0])
    def _():
        xb = _load_rows(xs_ref, 0, bm)
        h = jnp.dot(xb, w_gu_ref[...], preferred_element_type=jnp.float32) + b_gu_ref[...]
        h_glu = jnp.minimum(h[:, :d_expert], SWIGLU_LIMIT)
        h_lin = jnp.clip(h[:, d_expert:], -SWIGLU_LIMIT, SWIGLU_LIMIT)
        act = h_glu * jax.nn.sigmoid(SWIGLU_ALPHA * h_glu) * (h_lin + 1.0)
        y = jnp.dot(act, w_dn_ref[...], preferred_element_type=jnp.float32) + b_dn_ref[...]
        _store_rows(ys_ref, y)


def _expert_call(block_e, n_used, xs, w_gu, b_gu, w_dn, b_dn, n_blocks):
    d = w_gu.shape[1]
    d_expert = w_dn.shape[1]
    bm = ROW_BLOCK

    def row_map(m, be, nu):
        return (jnp.minimum(m, nu[0] - 1), 0)

    def exp_map3(m, be, nu):
        return (be[jnp.minimum(m, nu[0] - 1)], 0, 0)

    grid_spec = pltpu.PrefetchScalarGridSpec(
        num_scalar_prefetch=2,
        grid=(n_blocks,),
        in_specs=[
            pl.BlockSpec((bm * SUBLANES, LANES), row_map),
            pl.BlockSpec((None, d, 2 * d_expert), exp_map3),
            pl.BlockSpec((None, 1, 2 * d_expert), exp_map3),
            pl.BlockSpec((None, d_expert, d), exp_map3),
            pl.BlockSpec((None, 1, d), exp_map3),
        ],
        out_specs=pl.BlockSpec((bm * SUBLANES, LANES), row_map),
    )
    return pl.pallas_call(
        functools.partial(_expert_kernel, d_expert),
        grid_spec=grid_spec,
        out_shape=jax.ShapeDtypeStruct((n_blocks * bm * SUBLANES, LANES), jnp.float32),
        compiler_params=pltpu.CompilerParams(
            dimension_semantics=("arbitrary",),
            vmem_limit_bytes=VMEM_LIMIT_BYTES),
        name="expert_ffn",
    )(block_e, n_used, xs, w_gu, b_gu, w_dn, b_dn)


def _combine_kernel(alpha, slot8_ref, slot8_next_ref, ys_ref, x1t_ref, gates_ref, ln_g_ref, ln_b_ref,
                    out_ref, buf_a, buf_b, sem):
    step = pl.program_id(0)
    last = pl.num_programs(0) - 1
    tt = out_ref.shape[0] // 2

    def gather(slot_ref, half, buf, s):
        def row_copy(j, k):
            return pltpu.make_async_copy(_row_tile(ys_ref, slot_ref[(half * tt + j) * TOP_K + k]),
                                         _row_tile(buf, (k * tt + j) * SUBLANES), sem.at[s])
        return row_copy

    def reduce_tile(half, buf):
        rows = pl.ds(half * tt, tt)
        gates = gates_ref[rows, :]
        z = alpha * _load_rows(x1t_ref, half * tt, tt)
        for k in range(TOP_K):
            z = z + gates[:, k:k + 1] * _load_rows(buf, k * tt, tt)
        out_ref[rows, :] = _layer_norm(z, ln_g_ref[...], ln_b_ref[...])

    first_a = gather(slot8_ref, 0, buf_a, 0)
    second_b = gather(slot8_ref, 1, buf_b, 1)
    next_a = gather(slot8_next_ref, 0, buf_a, 0)

    @pl.when(step == 0)
    def _():
        _start_rows(first_a, tt)

    _start_rows(second_b, tt)
    _wait_rows(first_a, tt)
    reduce_tile(0, buf_a)

    @pl.when(step < last)
    def _():
        _start_rows(next_a, tt)

    _wait_rows(second_b, tt)
    reduce_tile(1, buf_b)


def _combine_call(slot8, ys, x1t, gates, ln_g, ln_b, alpha):
    n_tok = gates.shape[0]
    d = ln_g.shape[1]
    tt = TOKEN_TILE
    n_steps = n_tok // (2 * tt)
    return pl.pallas_call(
        functools.partial(_combine_kernel, alpha),
        grid=(n_steps,),
        in_specs=[
            pl.BlockSpec((2 * tt * TOP_K,), lambda i: (i,), memory_space=pltpu.SMEM),
            pl.BlockSpec((2 * tt * TOP_K,), lambda i: (jnp.minimum(i + 1, n_steps - 1),),
                         memory_space=pltpu.SMEM),
            pl.BlockSpec(memory_space=pl.ANY),
            pl.BlockSpec((2 * tt * SUBLANES, LANES), lambda i: (i, 0)),
            pl.BlockSpec((2 * tt, LANES), lambda i: (i, 0)),
            pl.BlockSpec((1, d), lambda i: (0, 0)),
            pl.BlockSpec((1, d), lambda i: (0, 0)),
        ],
        out_specs=pl.BlockSpec((2 * tt, d), lambda i: (i, 0)),
        out_shape=jax.ShapeDtypeStruct((n_tok, d), jnp.float32),
        scratch_shapes=[
            pltpu.VMEM((TOP_K * tt * SUBLANES, LANES), jnp.float32),
            pltpu.VMEM((TOP_K * tt * SUBLANES, LANES), jnp.float32),
            pltpu.SemaphoreType.DMA((2,)),
        ],
        compiler_params=pltpu.CompilerParams(
            dimension_semantics=("arbitrary",),
            vmem_limit_bytes=VMEM_LIMIT_BYTES),
        name="combine_ln2",
    )(slot8, slot8, ys, x1t, gates, ln_g, ln_b)


def _block_diag_gates(w_r, w_i):
    n_heads, dh, _ = w_r.shape
    per = GATE_GROUP // dh
    groups = []
    for gi in range(n_heads // per):
        def bd(w):
            m = jnp.zeros((GATE_GROUP, GATE_GROUP), w.dtype)
            for j in range(per):
                m = lax.dynamic_update_slice(m, w[gi * per + j], (j * dh, j * dh))
            return m
        groups.append(jnp.concatenate([bd(w_r), bd(w_i)], axis=1))
    return jnp.stack(groups)


def _layer(x, w_in, conv_w, lconv_w, lconv_b, w_r, b_r, w_i, b_i, lam, w_out, ln1_g, ln1_b,
           w_rt, b_rt, w_gu, b_gu, w_dn, b_dn, ln2_g, ln2_b, alpha):
    bsz, seq, d = x.shape
    assert d == SUBLANES * LANES, "tile-per-row layout needs D == 1024"
    n_tok = bsz * seq
    n_assign = n_tok * TOP_K
    n_blocks = -(-n_assign // ROW_BLOCK) + N_EXPERTS
    n_rows = n_blocks * ROW_BLOCK
    bf16 = jnp.bfloat16
    row = lambda v: v.reshape(1, -1)

    w_rt_p = jnp.zeros((d, LANES), jnp.float32).at[:, :N_EXPERTS].set(w_rt)
    w_rt_hi = w_rt_p.astype(bf16)
    w_rt_lo = (w_rt_p - w_rt_hi.astype(jnp.float32)).astype(bf16)
    w_rt_p = jnp.concatenate([w_rt_hi, w_rt_lo], axis=1)
    b_rt_p = jnp.zeros((1, LANES), jnp.float32).at[0, :N_EXPERTS].set(b_rt)
    x1t, route, gates, counts = _mixer_call(
        x, w_in.astype(bf16), conv_w, lconv_w, row(lconv_b), _block_diag_gates(w_r, w_i).astype(bf16),
        row(b_r), row(b_i), row(lam), w_out.astype(bf16), row(ln1_g), row(ln1_b), w_rt_p, b_rt_p, alpha)

    counts = counts[0, :N_EXPERTS]
    padded = (counts + ROW_BLOCK - 1) // ROW_BLOCK * ROW_BLOCK
    pad_end = jnp.cumsum(padded)
    pad_start = pad_end - padded
    top_e = route[:, :TOP_K]
    rank = route[:, TOP_K:2 * TOP_K]
    onehot = top_e[:, :, None] == jnp.arange(N_EXPERTS, dtype=jnp.int32)
    slot = jnp.sum(jnp.where(onehot, pad_start, 0), axis=-1) + rank
    slot8 = (slot.reshape(-1) * SUBLANES).astype(jnp.int32)
    block_row0 = jnp.arange(n_blocks, dtype=jnp.int32) * ROW_BLOCK
    block_e = jnp.minimum(
        jnp.sum((pad_end[None, :] <= block_row0[:, None]).astype(jnp.int32), axis=1), N_EXPERTS - 1)
    n_used = (pad_end[-1:] // ROW_BLOCK).astype(jnp.int32)
    zpos8 = jnp.concatenate([((pad_start + counts) * SUBLANES).astype(jnp.int32), n_used])

    xs = _dispatch_call(slot8, zpos8, x1t, n_rows)
    ys = _expert_call(block_e, n_used, xs, w_gu, b_gu[:, None, :], w_dn, b_dn[:, None, :], n_blocks)
    out = _combine_call(slot8, ys, x1t, gates, row(ln2_g), row(ln2_b), alpha)
    return out.reshape(bsz, seq, d)


def kernel(x, w_in, conv_w, lru_conv_w, lru_conv_b, w_rgate, b_rgate, w_igate, b_igate, lru_lambda,
           w_out, ln1_g, ln1_b, w_router, b_router, w_gate_up, b_gate_up, w_down, b_down, ln2_g, ln2_b):
    depth = w_in.shape[0]
    alpha = (2.0 * depth) ** 0.25
    for l in range(depth):
        x = _layer(x, w_in[l], conv_w[l], lru_conv_w[l], lru_conv_b[l], w_rgate[l], b_rgate[l],
                   w_igate[l], b_igate[l], lru_lambda[l], w_out[l], ln1_g[l], ln1_b[l],
                   w_router[l], b_router[l], w_gate_up[l], b_gate_up[l], w_down[l], b_down[l],
                   ln2_g[l], ln2_b[l], alpha)
    return x
```

```python
import functools

import jax
import jax.numpy as jnp
from jax import lax
from jax.experimental import pallas as pl
from jax.experimental.pallas import tpu as pltpu

SHORT_CONV_WIDTH = 3
LRU_CONV_WIDTH = 4
LRU_C = 8.0
N_EXPERTS = 32
TOP_K = 4
SWIGLU_LIMIT = 7.0
SWIGLU_ALPHA = 1.702
LN_EPS = 1e-5

LANES = 128
SUBLANES = 8
MXU_DIM = 256
VMEM_LIMIT_BYTES = 56 * 1024 * 1024

SEQ_TILE = 512
TOKEN_TILE = 256
ROW_BLOCK = 512
GATE_GROUP = MXU_DIM
ISSUE_UNROLL = 8


def _layer_norm(z, g, b):
    mu = jnp.mean(z, axis=-1, keepdims=True)
    zc = z - mu
    var = jnp.mean(zc * zc, axis=-1, keepdims=True)
    return zc * lax.rsqrt(var + LN_EPS) * g + b


def _load_rows(ref, base, n):
    return jnp.concatenate(
        [ref[pl.ds(base * SUBLANES + s, n, stride=SUBLANES), :] for s in range(SUBLANES)], axis=-1)


def _store_rows(ref, val):
    n = val.shape[0]
    for s in range(SUBLANES):
        ref[pl.ds(s, n, stride=SUBLANES), :] = val[:, s * LANES:(s + 1) * LANES]


def _row_tile(ref, row8):
    return ref.at[pl.ds(pl.multiple_of(row8, SUBLANES), SUBLANES), :]


def _start_rows(row_copy, n_tok):
    def body(g, c):
        for jj in range(ISSUE_UNROLL):
            for k in range(TOP_K):
                row_copy(g * ISSUE_UNROLL + jj, k).start(priority=k % 2)
        return c
    lax.fori_loop(0, n_tok // ISSUE_UNROLL, body, 0)


def _wait_rows(row_copy, n_tok):
    def body(g, c):
        for jj in range(ISSUE_UNROLL):
            for k in range(TOP_K):
                row_copy(g * ISSUE_UNROLL + jj, k).wait()
        return c
    lax.fori_loop(0, n_tok // ISSUE_UNROLL, body, 0)


def _mixer_kernel(alpha, d_conv, d_lru,
                  x_ref, w_in_ref, conv_w_ref, lconv_w_ref, lconv_b_ref, w_gate_ref,
                  b_r_ref, b_i_ref, lam_ref, w_out_ref, ln_g_ref, ln_b_ref, w_rt_ref, b_rt_ref,
                  x1t_ref, route_ref, gates_ref, counts_ref,
                  g_buf, rx_buf, h_carry, cnt_carry, tri_buf, a_buf, b_buf, hs_buf):
    ts = x_ref.shape[0]
    hdr = SUBLANES
    first_of_seq = pl.program_id(1) == 0
    first_step = jnp.logical_and(pl.program_id(0) == 0, first_of_seq)

    @pl.when(first_of_seq)
    def _():
        g_buf[0:hdr, :] = jnp.zeros((hdr, d_conv), jnp.float32)
        rx_buf[0:hdr, :] = jnp.zeros((hdr, d_lru), jnp.float32)
        h_carry[...] = jnp.zeros_like(h_carry)

    @pl.when(first_step)
    def _():
        cnt_carry[...] = jnp.zeros_like(cnt_carry)
        r = lax.broadcasted_iota(jnp.int32, (ts, ts), 0)
        c = lax.broadcasted_iota(jnp.int32, (ts, ts), 1)
        tri_buf[...] = (c < r).astype(jnp.bfloat16)

    x = x_ref[...]
    xb = x.astype(jnp.bfloat16)

    def proj(lo, hi):
        return jnp.dot(xb, w_in_ref[:, lo:hi], preferred_element_type=jnp.float32)

    c_b = proj(0, d_conv)
    g = proj(d_conv, 2 * d_conv) * proj(2 * d_conv, 3 * d_conv)
    g_buf[hdr:hdr + ts, :] = g
    y_conv = conv_w_ref[SHORT_CONV_WIDTH - 1:SHORT_CONV_WIDTH, :] * g
    for k in range(SHORT_CONV_WIDTH - 1):
        back = SHORT_CONV_WIDTH - 1 - k
        y_conv = y_conv + conv_w_ref[k:k + 1, :] * g_buf[hdr - back:hdr - back + ts, :]
    y_conv = c_b * y_conv
    g_buf[0:hdr, :] = g_buf[ts:ts + hdr, :]

    off = 3 * d_conv
    r_x = proj(off, off + d_lru)
    rx_buf[hdr:hdr + ts, :] = r_x
    u = lconv_w_ref[LRU_CONV_WIDTH - 1:LRU_CONV_WIDTH, :] * r_x + lconv_b_ref[...]
    for k in range(LRU_CONV_WIDTH - 1):
        back = LRU_CONV_WIDTH - 1 - k
        u = u + lconv_w_ref[k:k + 1, :] * rx_buf[hdr - back:hdr - back + ts, :]
    rx_buf[0:hdr, :] = rx_buf[ts:ts + hdr, :]

    ub = u.astype(jnp.bfloat16)
    r_parts, i_parts = [], []
    for gi in range(d_lru // GATE_GROUP):
        ri = jnp.dot(ub[:, gi * GATE_GROUP:(gi + 1) * GATE_GROUP], w_gate_ref[gi],
                     preferred_element_type=jnp.float32)
        r_parts.append(ri[:, :GATE_GROUP])
        i_parts.append(ri[:, GATE_GROUP:])
    r = jax.nn.sigmoid(jnp.concatenate(r_parts, axis=-1) + b_r_ref[...])
    i = jax.nn.sigmoid(jnp.concatenate(i_parts, axis=-1) + b_i_ref[...])

    neg_lam = -lam_ref[...]
    softplus = jnp.maximum(neg_lam, 0.0) + jnp.log1p(jnp.exp(-jnp.abs(neg_lam)))
    log_a = (-LRU_C) * r * softplus
    a = jnp.exp(log_a)
    bb = jnp.sqrt(jnp.tanh(-log_a) * (a * a + 1.0)) * (i * u)

    def scan_steps(a, b, n_rows, group):
        row = lax.broadcasted_iota(jnp.int32, (n_rows, d_lru), 0)
        if group < n_rows:
            row = row % group
        d = 1
        while d < group:
            keep = row >= d
            a_sh = jnp.where(keep, pltpu.roll(a, d, 0), 1.0)
            b_sh = jnp.where(keep, pltpu.roll(b, d, 0), 0.0)
            b = a * b_sh + b
            a = a * a_sh
            d *= 2
        return a, b

    n_grp = ts // SUBLANES
    n_col = d_lru // LANES
    a, bb = scan_steps(a, bb, ts, SUBLANES)
    for c in range(n_col):
        a_buf[c * ts:(c + 1) * ts, :] = a[:, c * LANES:(c + 1) * LANES]
        b_buf[c * ts:(c + 1) * ts, :] = bb[:, c * LANES:(c + 1) * LANES]

    def group_last(buf):
        return jnp.concatenate(
            [buf[pl.ds(c * ts + SUBLANES - 1, n_grp, stride=SUBLANES), :] for c in range(n_col)], axis=-1)

    grp_a, grp_b = scan_steps(group_last(a_buf), group_last(b_buf), n_grp, n_grp)
    grp_end = grp_a * h_carry[...] + grp_b
    grp_row = lax.broadcasted_iota(jnp.int32, (n_grp, d_lru), 0)
    grp_start = jnp.where(grp_row == 0, h_carry[...], pltpu.roll(grp_end, 1, 0))
    h_carry[...] = grp_end[n_grp - 1:n_grp, :]
    for c in range(n_col):
        for s in range(SUBLANES):
            hs_buf[pl.ds(c * ts + s, n_grp, stride=SUBLANES), :] = grp_start[:, c * LANES:(c + 1) * LANES]
    h = jnp.concatenate(
        [a_buf[c * ts:(c + 1) * ts, :] * hs_buf[c * ts:(c + 1) * ts, :] + b_buf[c * ts:(c + 1) * ts, :]
         for c in range(n_col)], axis=-1)

    r_g = proj(off + d_lru, off + 2 * d_lru)
    y_lru = h * jax.nn.gelu(r_g)

    mixed = (jnp.dot(y_conv.astype(jnp.bfloat16), w_out_ref[0:d_conv, :],
                     preferred_element_type=jnp.float32)
             + jnp.dot(y_lru.astype(jnp.bfloat16), w_out_ref[d_conv:d_conv + d_lru, :],
                       preferred_element_type=jnp.float32))
    x1 = _layer_norm(alpha * x + mixed, ln_g_ref[...], ln_b_ref[...])
    _store_rows(x1t_ref, x1)

    x1_hi = x1.astype(jnp.bfloat16)
    x1_lo = (x1 - x1_hi.astype(jnp.float32)).astype(jnp.bfloat16)
    hi_part = jnp.dot(x1_hi, w_rt_ref[...], preferred_element_type=jnp.float32)
    lo_part = jnp.dot(x1_lo, w_rt_ref[:, 0:LANES], preferred_element_type=jnp.float32)
    logits = hi_part[:, 0:LANES] + hi_part[:, LANES:2 * LANES] + lo_part + b_rt_ref[...]
    lane = lax.broadcasted_iota(jnp.int32, (ts, LANES), 1)
    lane_f = lane.astype(jnp.float32)
    neg_inf = jnp.float32(-jnp.inf)
    work = jnp.where(lane < N_EXPERTS, logits, neg_inf)
    sel = jnp.zeros((ts, LANES), jnp.bool_)
    top_v, top_e = [], []
    for _ in range(TOP_K):
        m = jnp.max(work, axis=-1, keepdims=True)
        e_f = jnp.min(jnp.where(work == m, lane_f, float(LANES)), axis=-1, keepdims=True)
        hit = lane_f == e_f
        sel = jnp.logical_or(sel, hit)
        work = jnp.where(hit, neg_inf, work)
        top_v.append(m)
        top_e.append(e_f.astype(jnp.int32))
    exps = [jnp.exp(v - top_v[0]) for v in top_v]
    denom = exps[0]
    for ex in exps[1:]:
        denom = denom + ex

    sel_f = sel.astype(jnp.float32)
    before = jnp.dot(tri_buf[...], sel_f.astype(jnp.bfloat16), preferred_element_type=jnp.float32)
    pos = before + cnt_carry[...]
    cnt_carry[...] = cnt_carry[...] + jnp.sum(sel_f, axis=0, keepdims=True)
    counts_ref[...] = cnt_carry[...].astype(jnp.int32)

    route = jnp.zeros((ts, LANES), jnp.int32)
    gates = jnp.zeros((ts, LANES), jnp.float32)
    for k in range(TOP_K):
        rank = jnp.sum(jnp.where(lane == top_e[k], pos, 0.0), axis=-1, keepdims=True)
        route = jnp.where(lane == k, top_e[k], route)
        route = jnp.where(lane == TOP_K + k, rank.astype(jnp.int32), route)
        gates = jnp.where(lane == k, exps[k] / denom, gates)
    route_ref[...] = route
    gates_ref[...] = gates


def _mixer_call(x, w_in, conv_w, lconv_w, lconv_b, w_gate, b_r, b_i, lam, w_out, ln_g, ln_b,
                w_rt, b_rt, alpha):
    bsz, seq, d = x.shape
    d_conv = conv_w.shape[1]
    d_lru = lconv_w.shape[1]
    ts = SEQ_TILE
    n_tok = bsz * seq
    nt = seq // ts

    def const(shape):
        return pl.BlockSpec(shape, lambda b, t: (0,) * len(shape))

    tok_map = lambda b, t: (b * nt + t, 0)
    kernel = functools.partial(_mixer_kernel, alpha, d_conv, d_lru)
    return pl.pallas_call(
        kernel,
        grid=(bsz, nt),
        in_specs=[
            pl.BlockSpec((None, ts, d), lambda b, t: (b, t, 0)),
            const(w_in.shape), const(conv_w.shape), const(lconv_w.shape), const(lconv_b.shape),
            const(w_gate.shape), const(b_r.shape), const(b_i.shape), const(lam.shape),
            const(w_out.shape), const(ln_g.shape), const(ln_b.shape), const(w_rt.shape),
            const(b_rt.shape),
        ],
        out_specs=[
            pl.BlockSpec((ts * SUBLANES, LANES), tok_map),
            pl.BlockSpec((ts, LANES), tok_map),
            pl.BlockSpec((ts, LANES), tok_map),
            pl.BlockSpec((1, LANES), lambda b, t: (0, 0)),
        ],
        out_shape=[
            jax.ShapeDtypeStruct((n_tok * SUBLANES, LANES), jnp.float32),
            jax.ShapeDtypeStruct((n_tok, LANES), jnp.int32),
            jax.ShapeDtypeStruct((n_tok, LANES), jnp.float32),
            jax.ShapeDtypeStruct((1, LANES), jnp.int32),
        ],
        scratch_shapes=[
            pltpu.VMEM((ts + SUBLANES, d_conv), jnp.float32),
            pltpu.VMEM((ts + SUBLANES, d_lru), jnp.float32),
            pltpu.VMEM((1, d_lru), jnp.float32),
            pltpu.VMEM((1, LANES), jnp.float32),
            pltpu.VMEM((ts, ts), jnp.bfloat16),
            pltpu.VMEM((ts * d_lru // LANES, LANES), jnp.float32),
            pltpu.VMEM((ts * d_lru // LANES, LANES), jnp.float32),
            pltpu.VMEM((ts * d_lru // LANES, LANES), jnp.float32),
        ],
        compiler_params=pltpu.CompilerParams(
            dimension_semantics=("arbitrary", "arbitrary"),
            vmem_limit_bytes=VMEM_LIMIT_BYTES),
        name="mixer_ln1_router",
    )(x, w_in, conv_w, lconv_w, lconv_b, w_gate, b_r, b_i, lam, w_out, ln_g, ln_b, w_rt, b_rt)


def _dispatch_kernel(slot8_ref, zpos8_ref, x_ref, xs_ref, zero_buf, sem, zsem):
    tt = x_ref.shape[0] // SUBLANES
    blk8 = zero_buf.shape[0]

    @pl.when(pl.program_id(0) == 0)
    def _():
        zero_buf[...] = jnp.zeros_like(zero_buf)

        def zcopy(e):
            start = pl.multiple_of(zpos8_ref[e], SUBLANES)
            return pltpu.make_async_copy(zero_buf, xs_ref.at[pl.ds(start, blk8), :], zsem)

        def zstart(e, c):
            zcopy(e).start()
            return c

        def zwait(e, c):
            zcopy(e).wait()
            return c

        lax.fori_loop(0, N_EXPERTS, zstart, 0)
        lax.fori_loop(0, N_EXPERTS, zwait, 0)

        def tcopy(b):
            return pltpu.make_async_copy(zero_buf, xs_ref.at[pl.ds(pl.multiple_of(b * blk8, blk8), blk8), :],
                                         zsem)

        def tstart(b, c):
            tcopy(b).start()
            return c

        def twait(b, c):
            tcopy(b).wait()
            return c

        n_used = zpos8_ref[N_EXPERTS]
        n_total = xs_ref.shape[0] // blk8
        lax.fori_loop(n_used, n_total, tstart, 0)
        lax.fori_loop(n_used, n_total, twait, 0)

    def row_copy(j, k):
        return pltpu.make_async_copy(_row_tile(x_ref, j * SUBLANES),
                                     _row_tile(xs_ref, slot8_ref[j * TOP_K + k]), sem)

    _start_rows(row_copy, tt)
    _wait_rows(row_copy, tt)


def _dispatch_call(slot8, zpos8, x1t, n_rows):
    n_tok = x1t.shape[0] // SUBLANES
    tt = TOKEN_TILE
    return pl.pallas_call(
        _dispatch_kernel,
        grid=(n_tok // tt,),
        in_specs=[
            pl.BlockSpec((tt * TOP_K,), lambda i: (i,), memory_space=pltpu.SMEM),
            pl.BlockSpec(memory_space=pltpu.SMEM),
            pl.BlockSpec((tt * SUBLANES, LANES), lambda i: (i, 0)),
        ],
        out_specs=pl.BlockSpec(memory_space=pl.ANY),
        out_shape=jax.ShapeDtypeStruct(((n_rows + ROW_BLOCK) * SUBLANES, LANES), jnp.float32),
        scratch_shapes=[
            pltpu.VMEM((ROW_BLOCK * SUBLANES, LANES), jnp.float32),
            pltpu.SemaphoreType.DMA,
            pltpu.SemaphoreType.DMA,
        ],
        compiler_params=pltpu.CompilerParams(
            dimension_semantics=("arbitrary",),
            vmem_limit_bytes=VMEM_LIMIT_BYTES),
        name="dispatch_rows",
    )(slot8, zpos8, x1t)


def _expert_kernel(d_expert, plan_ref, xs_ref, w_gu_hbm, b_gu_ref, w_dn_hbm, b_dn_ref, ys_ref,
                   wgu_buf, wdn_buf, sem):
    m = pl.program_id(0)
    bm = xs_ref.shape[0] // SUBLANES

    def fetch(e, s):
        return (pltpu.make_async_copy(w_gu_hbm.at[e], wgu_buf.at[s], sem.at[0, s]),
                pltpu.make_async_copy(w_dn_hbm.at[e], wdn_buf.at[s], sem.at[1, s]))

    @pl.when(m < plan_ref[4, 0])
    def _():
        e = plan_ref[0, m]
        s = plan_ref[3, m]

        @pl.when(m == 0)
        def _():
            for c in fetch(e, s):
                c.start()

        @pl.when(plan_ref[1, m] == 1)
        def _():
            nxt = plan_ref[2, m]

            @pl.when(nxt >= 0)
            def _():
                for c in fetch(nxt, 1 - s):
                    c.start()

            for c in fetch(e, s):
                c.wait()

        xb = _load_rows(xs_ref, 0, bm)
        h = jnp.dot(xb, wgu_buf[s], preferred_element_type=jnp.float32) + b_gu_ref[pl.ds(e, 1), :]
        h_glu = jnp.minimum(h[:, :d_expert], SWIGLU_LIMIT)
        h_lin = jnp.clip(h[:, d_expert:], -SWIGLU_LIMIT, SWIGLU_LIMIT)
        act = h_glu * jax.nn.sigmoid(SWIGLU_ALPHA * h_glu) * (h_lin + 1.0)
        y = jnp.dot(act, wdn_buf[s], preferred_element_type=jnp.float32) + b_dn_ref[pl.ds(e, 1), :]
        _store_rows(ys_ref, y)


def _expert_call(plan, xs, w_gu, b_gu, w_dn, b_dn, n_blocks):
    n_exp, d, two_de = w_gu.shape
    d_expert = w_dn.shape[1]
    bm = ROW_BLOCK

    def row_map(m, plan_ref):
        return (jnp.minimum(m, plan_ref[4, 0] - 1), 0)

    grid_spec = pltpu.PrefetchScalarGridSpec(
        num_scalar_prefetch=1,
        grid=(n_blocks,),
        in_specs=[
            pl.BlockSpec((bm * SUBLANES, LANES), row_map),
            pl.BlockSpec(memory_space=pl.ANY),
            pl.BlockSpec((n_exp, two_de), lambda m, plan_ref: (0, 0)),
            pl.BlockSpec(memory_space=pl.ANY),
            pl.BlockSpec((n_exp, d), lambda m, plan_ref: (0, 0)),
        ],
        out_specs=pl.BlockSpec((bm * SUBLANES, LANES), row_map),
        scratch_shapes=[
            pltpu.VMEM((2, d, two_de), jnp.float32),
            pltpu.VMEM((2, d_expert, d), jnp.float32),
            pltpu.SemaphoreType.DMA((2, 2)),
        ],
    )
    return pl.pallas_call(
        functools.partial(_expert_kernel, d_expert),
        grid_spec=grid_spec,
        out_shape=jax.ShapeDtypeStruct((n_blocks * bm * SUBLANES, LANES), jnp.float32),
        compiler_params=pltpu.CompilerParams(
            dimension_semantics=("arbitrary",),
            vmem_limit_bytes=VMEM_LIMIT_BYTES),
        name="expert_ffn",
    )(plan, xs, w_gu, b_gu, w_dn, b_dn)


def _combine_kernel(alpha, slot8_ref, slot8_next_ref, ys_ref, x1t_ref, gates_ref, ln_g_ref, ln_b_ref,
                    out_ref, buf_a, buf_b, sem):
    step = pl.program_id(0)
    last = pl.num_programs(0) - 1
    tt = out_ref.shape[0] // 2

    def gather(slot_ref, half, buf, s):
        def row_copy(j, k):
            return pltpu.make_async_copy(_row_tile(ys_ref, slot_ref[(half * tt + j) * TOP_K + k]),
                                         _row_tile(buf, (k * tt + j) * SUBLANES), sem.at[s])
        return row_copy

    def reduce_tile(half, buf):
        rows = pl.ds(half * tt, tt)
        gates = gates_ref[rows, :]
        z = alpha * _load_rows(x1t_ref, half * tt, tt)
        for k in range(TOP_K):
            z = z + gates[:, k:k + 1] * _load_rows(buf, k * tt, tt)
        out_ref[rows, :] = _layer_norm(z, ln_g_ref[...], ln_b_ref[...])

    first_a = gather(slot8_ref, 0, buf_a, 0)
    second_b = gather(slot8_ref, 1, buf_b, 1)
    next_a = gather(slot8_next_ref, 0, buf_a, 0)

    @pl.when(step == 0)
    def _():
        _start_rows(first_a, tt)

    _start_rows(second_b, tt)
    _wait_rows(first_a, tt)
    reduce_tile(0, buf_a)

    @pl.when(step < last)
    def _():
        _start_rows(next_a, tt)

    _wait_rows(second_b, tt)
    reduce_tile(1, buf_b)


def _combine_call(slot8, ys, x1t, gates, ln_g, ln_b, alpha):
    n_tok = gates.shape[0]
    d = ln_g.shape[1]
    tt = TOKEN_TILE
    n_steps = n_tok // (2 * tt)
    return pl.pallas_call(
        functools.partial(_combine_kernel, alpha),
        grid=(n_steps,),
        in_specs=[
            pl.BlockSpec((2 * tt * TOP_K,), lambda i: (i,), memory_space=pltpu.SMEM),
            pl.BlockSpec((2 * tt * TOP_K,), lambda i: (jnp.minimum(i + 1, n_steps - 1),),
                         memory_space=pltpu.SMEM),
            pl.BlockSpec(memory_space=pl.ANY),
            pl.BlockSpec((2 * tt * SUBLANES, LANES), lambda i: (i, 0)),
            pl.BlockSpec((2 * tt, LANES), lambda i: (i, 0)),
            pl.BlockSpec((1, d), lambda i: (0, 0)),
            pl.BlockSpec((1, d), lambda i: (0, 0)),
        ],
        out_specs=pl.BlockSpec((2 * tt, d), lambda i: (i, 0)),
        out_shape=jax.ShapeDtypeStruct((n_tok, d), jnp.float32),
        scratch_shapes=[
            pltpu.VMEM((TOP_K * tt * SUBLANES, LANES), jnp.float32),
            pltpu.VMEM((TOP_K * tt * SUBLANES, LANES), jnp.float32),
            pltpu.SemaphoreType.DMA((2,)),
        ],
        compiler_params=pltpu.CompilerParams(
            dimension_semantics=("arbitrary",),
            vmem_limit_bytes=VMEM_LIMIT_BYTES),
        name="combine_ln2",
    )(slot8, slot8, ys, x1t, gates, ln_g, ln_b)


def _block_diag_gates(w_r, w_i):
    n_heads, dh, _ = w_r.shape
    per = GATE_GROUP // dh
    groups = []
    for gi in range(n_heads // per):
        def bd(w):
            m = jnp.zeros((GATE_GROUP, GATE_GROUP), w.dtype)
            for j in range(per):
                m = lax.dynamic_update_slice(m, w[gi * per + j], (j * dh, j * dh))
            return m
        groups.append(jnp.concatenate([bd(w_r), bd(w_i)], axis=1))
    return jnp.stack(groups)


def _layer(x, w_in, conv_w, lconv_w, lconv_b, w_r, b_r, w_i, b_i, lam, w_out, ln1_g, ln1_b,
           w_rt, b_rt, w_gu, b_gu, w_dn, b_dn, ln2_g, ln2_b, alpha):
    bsz, seq, d = x.shape
    assert d == SUBLANES * LANES, "tile-per-row layout needs D == 1024"
    n_tok = bsz * seq
    n_assign = n_tok * TOP_K
    n_blocks = -(-n_assign // ROW_BLOCK) + N_EXPERTS
    n_rows = n_blocks * ROW_BLOCK
    bf16 = jnp.bfloat16
    row = lambda v: v.reshape(1, -1)

    w_rt_p = jnp.zeros((d, LANES), jnp.float32).at[:, :N_EXPERTS].set(w_rt)
    w_rt_hi = w_rt_p.astype(bf16)
    w_rt_lo = (w_rt_p - w_rt_hi.astype(jnp.float32)).astype(bf16)
    w_rt_p = jnp.concatenate([w_rt_hi, w_rt_lo], axis=1)
    b_rt_p = jnp.zeros((1, LANES), jnp.float32).at[0, :N_EXPERTS].set(b_rt)
    x1t, route, gates, counts = _mixer_call(
        x, w_in.astype(bf16), conv_w, lconv_w, row(lconv_b), _block_diag_gates(w_r, w_i).astype(bf16),
        row(b_r), row(b_i), row(lam), w_out.astype(bf16), row(ln1_g), row(ln1_b), w_rt_p, b_rt_p, alpha)

    counts = counts[0, :N_EXPERTS]
    padded = (counts + ROW_BLOCK - 1) // ROW_BLOCK * ROW_BLOCK
    pad_end = jnp.cumsum(padded)
    pad_start = pad_end - padded
    top_e = route[:, :TOP_K]
    rank = route[:, TOP_K:2 * TOP_K]
    onehot = top_e[:, :, None] == jnp.arange(N_EXPERTS, dtype=jnp.int32)
    slot = jnp.sum(jnp.where(onehot, pad_start, 0), axis=-1) + rank
    slot8 = (slot.reshape(-1) * SUBLANES).astype(jnp.int32)
    block_row0 = jnp.arange(n_blocks, dtype=jnp.int32) * ROW_BLOCK
    block_e = jnp.minimum(
        jnp.sum((pad_end[None, :] <= block_row0[:, None]).astype(jnp.int32), axis=1), N_EXPERTS - 1)
    n_used = (pad_end[-1:] // ROW_BLOCK).astype(jnp.int32)
    zpos8 = jnp.concatenate([((pad_start + counts) * SUBLANES).astype(jnp.int32), n_used])

    experts = jnp.arange(N_EXPERTS, dtype=jnp.int32)
    nonempty = counts > 0
    later_nonempty = jnp.where(nonempty[None, :] & (experts[None, :] > experts[:, None]),
                               experts[None, :], N_EXPERTS)
    next_tbl = jnp.min(later_nonempty, axis=1)
    next_tbl = jnp.where(next_tbl == N_EXPERTS, -1, next_tbl)
    buf_tbl = (jnp.cumsum(nonempty.astype(jnp.int32)) - 1) % 2
    is_e = block_e[:, None] == experts[None, :]
    look = lambda tbl: jnp.sum(jnp.where(is_e, tbl[None, :], 0), axis=1)
    first = (block_row0 == look(pad_start)).astype(jnp.int32)
    last_row = jnp.zeros((n_blocks,), jnp.int32).at[0].set(n_used[0])
    plan = jnp.stack([block_e, first, look(next_tbl), look(buf_tbl), last_row]).astype(jnp.int32)

    xs = _dispatch_call(slot8, zpos8, x1t, n_rows)
    ys = _expert_call(plan, xs, w_gu, b_gu, w_dn, b_dn, n_blocks)
    out = _combine_call(slot8, ys, x1t, gates, row(ln2_g), row(ln2_b), alpha)
    return out.reshape(bsz, seq, d)


def kernel(x, w_in, conv_w, lru_conv_w, lru_conv_b, w_rgate, b_rgate, w_igate, b_igate, lru_lambda,
           w_out, ln1_g, ln1_b, w_router, b_router, w_gate_up, b_gate_up, w_down, b_down, ln2_g, ln2_b):
    depth = w_in.shape[0]
    alpha = (2.0 * depth) ** 0.25
    for l in range(depth):
        x = _layer(x, w_in[l], conv_w[l], lru_conv_w[l], lru_conv_b[l], w_rgate[l], b_rgate[l],
                   w_igate[l], b_igate[l], lru_lambda[l], w_out[l], ln1_g[l], ln1_b[l],
                   w_router[l], b_router[l], w_gate_up[l], b_gate_up[l], w_down[l], b_down[l],
                   ln2_g[l], ln2_b[l], alpha)
    return x
```

```python
import functools

import jax
import jax.numpy as jnp
from jax import lax
from jax.experimental import pallas as pl
from jax.experimental.pallas import tpu as pltpu

SHORT_CONV_WIDTH = 3
LRU_CONV_WIDTH = 4
LRU_C = 8.0
N_EXPERTS = 32
TOP_K = 4
SWIGLU_LIMIT = 7.0
SWIGLU_ALPHA = 1.702
LN_EPS = 1e-5

LANES = 128
SUBLANES = 8
MXU_DIM = 256
VMEM_LIMIT_BYTES = 56 * 1024 * 1024

SEQ_TILE = 512
TOKEN_TILE = 256
ROW_BLOCK = 512
GATE_GROUP = MXU_DIM
ISSUE_UNROLL = 8


def _layer_norm(z, g, b):
    mu = jnp.mean(z, axis=-1, keepdims=True)
    zc = z - mu
    var = jnp.mean(zc * zc, axis=-1, keepdims=True)
    return zc * lax.rsqrt(var + LN_EPS) * g + b


def _load_rows(ref, base, n):
    return jnp.concatenate(
        [ref[pl.ds(base * SUBLANES + s, n, stride=SUBLANES), :] for s in range(SUBLANES)], axis=-1)


def _store_rows(ref, val):
    n = val.shape[0]
    for s in range(SUBLANES):
        ref[pl.ds(s, n, stride=SUBLANES), :] = val[:, s * LANES:(s + 1) * LANES]


def _row_tile(ref, row8):
    return ref.at[pl.ds(pl.multiple_of(row8, SUBLANES), SUBLANES), :]


def _start_rows(row_copy, n_tok):
    def body(g, c):
        for jj in range(ISSUE_UNROLL):
            for k in range(TOP_K):
                row_copy(g * ISSUE_UNROLL + jj, k).start(priority=k % 2)
        return c
    lax.fori_loop(0, n_tok // ISSUE_UNROLL, body, 0)


def _wait_rows(row_copy, n_tok):
    def body(g, c):
        for jj in range(ISSUE_UNROLL):
            for k in range(TOP_K):
                row_copy(g * ISSUE_UNROLL + jj, k).wait()
        return c
    lax.fori_loop(0, n_tok // ISSUE_UNROLL, body, 0)


def _mixer_kernel(alpha, d_conv, d_lru,
                  x_ref, w_in_ref, conv_w_ref, lconv_w_ref, lconv_b_ref, w_gate_ref,
                  b_r_ref, b_i_ref, lam_ref, w_out_ref, ln_g_ref, ln_b_ref, w_rt_ref, b_rt_ref,
                  route_ref, gates_ref, counts_ref, x1t_hbm, xs_hbm,
                  g_buf, rx_buf, h_carry, cnt_carry, tri_buf, a_buf, b_buf, hs_buf,
                  stage, slot_vmem, slot_smem, cnt_vmem, cnt_smem, zero_buf,
                  row_sem, x1_sem, smem_sem, zero_sem):
    ts = x_ref.shape[0]
    hdr = SUBLANES
    first_of_seq = pl.program_id(1) == 0
    first_step = jnp.logical_and(pl.program_id(0) == 0, first_of_seq)
    step = pl.program_id(0) * pl.num_programs(1) + pl.program_id(1)
    last_step = step == pl.num_programs(0) * pl.num_programs(1) - 1
    par = step % 2
    cap = xs_hbm.shape[0] // (SUBLANES * N_EXPERTS)

    @pl.when(first_of_seq)
    def _():
        g_buf[0:hdr, :] = jnp.zeros((hdr, d_conv), jnp.float32)
        rx_buf[0:hdr, :] = jnp.zeros((hdr, d_lru), jnp.float32)
        h_carry[...] = jnp.zeros_like(h_carry)

    @pl.when(first_step)
    def _():
        cnt_carry[...] = jnp.zeros_like(cnt_carry)
        r = lax.broadcasted_iota(jnp.int32, (ts, ts), 0)
        c = lax.broadcasted_iota(jnp.int32, (ts, ts), 1)
        tri_buf[...] = (c < r).astype(jnp.bfloat16)

    x = x_ref[...]
    xb = x.astype(jnp.bfloat16)

    def proj(lo, hi):
        return jnp.dot(xb, w_in_ref[:, lo:hi], preferred_element_type=jnp.float32)

    c_b = proj(0, d_conv)
    g = proj(d_conv, 2 * d_conv) * proj(2 * d_conv, 3 * d_conv)
    g_buf[hdr:hdr + ts, :] = g
    y_conv = conv_w_ref[SHORT_CONV_WIDTH - 1:SHORT_CONV_WIDTH, :] * g
    for k in range(SHORT_CONV_WIDTH - 1):
        back = SHORT_CONV_WIDTH - 1 - k
        y_conv = y_conv + conv_w_ref[k:k + 1, :] * g_buf[hdr - back:hdr - back + ts, :]
    y_conv = c_b * y_conv
    g_buf[0:hdr, :] = g_buf[ts:ts + hdr, :]

    off = 3 * d_conv
    r_x = proj(off, off + d_lru)
    rx_buf[hdr:hdr + ts, :] = r_x
    u = lconv_w_ref[LRU_CONV_WIDTH - 1:LRU_CONV_WIDTH, :] * r_x + lconv_b_ref[...]
    for k in range(LRU_CONV_WIDTH - 1):
        back = LRU_CONV_WIDTH - 1 - k
        u = u + lconv_w_ref[k:k + 1, :] * rx_buf[hdr - back:hdr - back + ts, :]
    rx_buf[0:hdr, :] = rx_buf[ts:ts + hdr, :]

    ub = u.astype(jnp.bfloat16)
    r_parts, i_parts = [], []
    for gi in range(d_lru // GATE_GROUP):
        ri = jnp.dot(ub[:, gi * GATE_GROUP:(gi + 1) * GATE_GROUP], w_gate_ref[gi],
                     preferred_element_type=jnp.float32)
        r_parts.append(ri[:, :GATE_GROUP])
        i_parts.append(ri[:, GATE_GROUP:])
    r = jax.nn.sigmoid(jnp.concatenate(r_parts, axis=-1) + b_r_ref[...])
    i = jax.nn.sigmoid(jnp.concatenate(i_parts, axis=-1) + b_i_ref[...])

    neg_lam = -lam_ref[...]
    softplus = jnp.maximum(neg_lam, 0.0) + jnp.log1p(jnp.exp(-jnp.abs(neg_lam)))
    log_a = (-LRU_C) * r * softplus
    a = jnp.exp(log_a)
    bb = jnp.sqrt(jnp.tanh(-log_a) * (a * a + 1.0)) * (i * u)

    def scan_steps(a, b, n_rows, group):
        row = lax.broadcasted_iota(jnp.int32, (n_rows, d_lru), 0)
        if group < n_rows:
            row = row % group
        d = 1
        while d < group:
            keep = row >= d
            a_sh = jnp.where(keep, pltpu.roll(a, d, 0), 1.0)
            b_sh = jnp.where(keep, pltpu.roll(b, d, 0), 0.0)
            b = a * b_sh + b
            a = a * a_sh
            d *= 2
        return a, b

    n_grp = ts // SUBLANES
    n_col = d_lru // LANES
    a, bb = scan_steps(a, bb, ts, SUBLANES)
    for c in range(n_col):
        a_buf[c * ts:(c + 1) * ts, :] = a[:, c * LANES:(c + 1) * LANES]
        b_buf[c * ts:(c + 1) * ts, :] = bb[:, c * LANES:(c + 1) * LANES]

    def group_last(buf):
        return jnp.concatenate(
            [buf[pl.ds(c * ts + SUBLANES - 1, n_grp, stride=SUBLANES), :] for c in range(n_col)], axis=-1)

    grp_a, grp_b = scan_steps(group_last(a_buf), group_last(b_buf), n_grp, n_grp)
    grp_end = grp_a * h_carry[...] + grp_b
    grp_row = lax.broadcasted_iota(jnp.int32, (n_grp, d_lru), 0)
    grp_start = jnp.where(grp_row == 0, h_carry[...], pltpu.roll(grp_end, 1, 0))
    h_carry[...] = grp_end[n_grp - 1:n_grp, :]
    for c in range(n_col):
        for s in range(SUBLANES):
            hs_buf[pl.ds(c * ts + s, n_grp, stride=SUBLANES), :] = grp_start[:, c * LANES:(c + 1) * LANES]
    h = jnp.concatenate(
        [a_buf[c * ts:(c + 1) * ts, :] * hs_buf[c * ts:(c + 1) * ts, :] + b_buf[c * ts:(c + 1) * ts, :]
         for c in range(n_col)], axis=-1)

    r_g = proj(off + d_lru, off + 2 * d_lru)
    y_lru = h * jax.nn.gelu(r_g)

    mixed = (jnp.dot(y_conv.astype(jnp.bfloat16), w_out_ref[0:d_conv, :],
                     preferred_element_type=jnp.float32)
             + jnp.dot(y_lru.astype(jnp.bfloat16), w_out_ref[d_conv:d_conv + d_lru, :],
                       preferred_element_type=jnp.float32))
    x1 = _layer_norm(alpha * x + mixed, ln_g_ref[...], ln_b_ref[...])
    _store_rows(stage.at[par], x1)

    x1_hi = x1.astype(jnp.bfloat16)
    x1_lo = (x1 - x1_hi.astype(jnp.float32)).astype(jnp.bfloat16)
    hi_part = jnp.dot(x1_hi, w_rt_ref[...], preferred_element_type=jnp.float32)
    lo_part = jnp.dot(x1_lo, w_rt_ref[:, 0:LANES], preferred_element_type=jnp.float32)
    logits = hi_part[:, 0:LANES] + hi_part[:, LANES:2 * LANES] + lo_part + b_rt_ref[...]
    lane = lax.broadcasted_iota(jnp.int32, (ts, LANES), 1)
    lane_f = lane.astype(jnp.float32)
    neg_inf = jnp.float32(-jnp.inf)
    work = jnp.where(lane < N_EXPERTS, logits, neg_inf)
    sel = jnp.zeros((ts, LANES), jnp.bool_)
    top_v, top_e = [], []
    for _ in range(TOP_K):
        m = jnp.max(work, axis=-1, keepdims=True)
        e_f = jnp.min(jnp.where(work == m, lane_f, float(LANES)), axis=-1, keepdims=True)
        hit = lane_f == e_f
        sel = jnp.logical_or(sel, hit)
        work = jnp.where(hit, neg_inf, work)
        top_v.append(m)
        top_e.append(e_f)
    exps = [jnp.exp(v - top_v[0]) for v in top_v]
    denom = exps[0]
    for ex in exps[1:]:
        denom = denom + ex

    sel_f = sel.astype(jnp.float32)
    before = jnp.dot(tri_buf[...], sel_f.astype(jnp.bfloat16), preferred_element_type=jnp.float32)
    pos = before + cnt_carry[...]
    cnt_carry[...] = cnt_carry[...] + jnp.sum(sel_f, axis=0, keepdims=True)
    counts = cnt_carry[...].astype(jnp.int32)
    counts_ref[...] = counts

    route = jnp.zeros((ts, LANES), jnp.float32)
    gates = jnp.zeros((ts, LANES), jnp.float32)
    for k in range(TOP_K):
        rank = jnp.sum(jnp.where(lane_f == top_e[k], pos, 0.0), axis=-1, keepdims=True)
        route = jnp.where(lane == k, top_e[k], route)
        route = jnp.where(lane == TOP_K + k, rank, route)
        gates = jnp.where(lane == k, exps[k] / denom, gates)
    gates_ref[...] = gates
    route_t = jnp.transpose(route)[0:2 * TOP_K, :].astype(jnp.int32)
    route_ref[...] = route_t

    slots8 = (route_t[0:TOP_K, :] * cap + route_t[TOP_K:2 * TOP_K, :]) * SUBLANES
    slot_vmem[...] = jnp.concatenate([slots8, jnp.zeros_like(slots8)], axis=0)
    def rows_of(p):
        def row_copy(j, k):
            return pltpu.make_async_copy(_row_tile(stage.at[p], j * SUBLANES),
                                         _row_tile(xs_hbm, slot_smem[p * SUBLANES + k, j]), row_sem.at[p])
        return row_copy

    def x1_copy(p, s):
        rows = ts * SUBLANES
        return pltpu.make_async_copy(stage.at[p], x1t_hbm.at[pl.ds(pl.multiple_of(s * rows, rows), rows), :],
                                     x1_sem.at[p])

    def dispatch_from(p):
        to_smem = pltpu.make_async_copy(
            slot_vmem, slot_smem.at[pl.ds(p * SUBLANES, SUBLANES), :], smem_sem)
        to_smem.start()

        @pl.when(step > 0)
        def _():
            _wait_rows(rows_of(1 - p), ts)
            x1_copy(1 - p, step - 1).wait()

        to_smem.wait()
        _start_rows(rows_of(p), ts)
        x1_copy(p, step).start()

        @pl.when(last_step)
        def _():
            _wait_rows(rows_of(p), ts)
            x1_copy(p, step).wait()

    for p in range(2):
        pl.when(par == p)(functools.partial(dispatch_from, p))

    @pl.when(last_step)
    def _():
        zero_buf[...] = jnp.zeros_like(zero_buf)
        cnt_vmem[...] = jnp.broadcast_to(counts, cnt_vmem.shape)
        cnt_copy = pltpu.make_async_copy(cnt_vmem, cnt_smem, smem_sem)
        cnt_copy.start()
        cnt_copy.wait()

        def zcopy(e):
            start = pl.multiple_of((e * cap + cnt_smem[0, e]) * SUBLANES, SUBLANES)
            return pltpu.make_async_copy(zero_buf, xs_hbm.at[pl.ds(start, zero_buf.shape[0]), :], zero_sem)

        def zstart(e, c):
            zcopy(e).start()
            return c

        def zwait(e, c):
            zcopy(e).wait()
            return c

        lax.fori_loop(0, N_EXPERTS, zstart, 0)
        lax.fori_loop(0, N_EXPERTS, zwait, 0)


def _mixer_call(x, w_in, conv_w, lconv_w, lconv_b, w_gate, b_r, b_i, lam, w_out, ln_g, ln_b,
                w_rt, b_rt, alpha, expert_cap):
    bsz, seq, d = x.shape
    d_conv = conv_w.shape[1]
    d_lru = lconv_w.shape[1]
    ts = SEQ_TILE
    n_tok = bsz * seq
    nt = seq // ts

    def const(shape):
        return pl.BlockSpec(shape, lambda b, t: (0,) * len(shape))

    tok_map = lambda b, t: (b * nt + t, 0)
    kernel = functools.partial(_mixer_kernel, alpha, d_conv, d_lru)
    return pl.pallas_call(
        kernel,
        grid=(bsz, nt),
        in_specs=[
            pl.BlockSpec((None, ts, d), lambda b, t: (b, t, 0)),
            const(w_in.shape), const(conv_w.shape), const(lconv_w.shape), const(lconv_b.shape),
            const(w_gate.shape), const(b_r.shape), const(b_i.shape), const(lam.shape),
            const(w_out.shape), const(ln_g.shape), const(ln_b.shape), const(w_rt.shape),
            const(b_rt.shape),
        ],
        out_specs=[
            pl.BlockSpec((2 * TOP_K, ts), lambda b, t: (0, b * nt + t)),
            pl.BlockSpec((ts, LANES), tok_map),
            pl.BlockSpec((1, LANES), lambda b, t: (0, 0)),
            pl.BlockSpec(memory_space=pl.ANY),
            pl.BlockSpec(memory_space=pl.ANY),
        ],
        out_shape=[
            jax.ShapeDtypeStruct((2 * TOP_K, n_tok), jnp.int32),
            jax.ShapeDtypeStruct((n_tok, LANES), jnp.float32),
            jax.ShapeDtypeStruct((1, LANES), jnp.int32),
            jax.ShapeDtypeStruct((n_tok * SUBLANES, LANES), jnp.float32),
            jax.ShapeDtypeStruct((N_EXPERTS * expert_cap * SUBLANES, LANES), jnp.float32),
        ],
        scratch_shapes=[
            pltpu.VMEM((ts + SUBLANES, d_conv), jnp.float32),
            pltpu.VMEM((ts + SUBLANES, d_lru), jnp.float32),
            pltpu.VMEM((1, d_lru), jnp.float32),
            pltpu.VMEM((1, LANES), jnp.float32),
            pltpu.VMEM((ts, ts), jnp.bfloat16),
            pltpu.VMEM((ts * d_lru // LANES, LANES), jnp.float32),
            pltpu.VMEM((ts * d_lru // LANES, LANES), jnp.float32),
            pltpu.VMEM((ts * d_lru // LANES, LANES), jnp.float32),
            pltpu.VMEM((2, ts * SUBLANES, LANES), jnp.float32),
            pltpu.VMEM((SUBLANES, ts), jnp.int32),
            pltpu.SMEM((2 * SUBLANES, ts), jnp.int32),
            pltpu.VMEM((SUBLANES, LANES), jnp.int32),
            pltpu.SMEM((SUBLANES, LANES), jnp.int32),
            pltpu.VMEM((ROW_BLOCK * SUBLANES, LANES), jnp.float32),
            pltpu.SemaphoreType.DMA((2,)),
            pltpu.SemaphoreType.DMA((2,)),
            pltpu.SemaphoreType.DMA,
            pltpu.SemaphoreType.DMA,
        ],
        compiler_params=pltpu.CompilerParams(
            dimension_semantics=("arbitrary", "arbitrary"),
            vmem_limit_bytes=VMEM_LIMIT_BYTES),
        name="mixer_ln1_router",
    )(x, w_in, conv_w, lconv_w, lconv_b, w_gate, b_r, b_i, lam, w_out, ln_g, ln_b, w_rt, b_rt)


def _expert_kernel(d_expert, plan_ref, xs_ref, w_gu_hbm, b_gu_ref, w_dn_hbm, b_dn_ref, ys_ref,
                   wgu_buf, wdn_buf, sem):
    m = pl.program_id(0)
    bm = xs_ref.shape[0] // SUBLANES

    def fetch(e, s):
        return (pltpu.make_async_copy(w_gu_hbm.at[e], wgu_buf.at[s], sem.at[0, s]),
                pltpu.make_async_copy(w_dn_hbm.at[e], wdn_buf.at[s], sem.at[1, s]))

    @pl.when(m < plan_ref[4, 0])
    def _():
        e = plan_ref[0, m]
        s = plan_ref[3, m]

        @pl.when(m == 0)
        def _():
            for c in fetch(e, s):
                c.start()

        @pl.when(plan_ref[1, m] == 1)
        def _():
            nxt = plan_ref[2, m]

            @pl.when(nxt >= 0)
            def _():
                for c in fetch(nxt, 1 - s):
                    c.start()

            for c in fetch(e, s):
                c.wait()

        xb = _load_rows(xs_ref, 0, bm)
        h = jnp.dot(xb, wgu_buf[s], preferred_element_type=jnp.float32) + b_gu_ref[pl.ds(e, 1), :]
        h_glu = jnp.minimum(h[:, :d_expert], SWIGLU_LIMIT)
        h_lin = jnp.clip(h[:, d_expert:], -SWIGLU_LIMIT, SWIGLU_LIMIT)
        act = h_glu * jax.nn.sigmoid(SWIGLU_ALPHA * h_glu) * (h_lin + 1.0)
        y = jnp.dot(act, wdn_buf[s], preferred_element_type=jnp.float32) + b_dn_ref[pl.ds(e, 1), :]
        _store_rows(ys_ref, y)


def _expert_call(plan, xs, w_gu, b_gu, w_dn, b_dn, n_blocks):
    n_exp, d, two_de = w_gu.shape
    d_expert = w_dn.shape[1]
    bm = ROW_BLOCK

    def row_map(m, plan_ref):
        return (jnp.minimum(m, plan_ref[4, 0] - 1), 0)

    def xs_map(m, plan_ref):
        return (plan_ref[5, jnp.minimum(m, plan_ref[4, 0] - 1)], 0)

    grid_spec = pltpu.PrefetchScalarGridSpec(
        num_scalar_prefetch=1,
        grid=(n_blocks,),
        in_specs=[
            pl.BlockSpec((bm * SUBLANES, LANES), xs_map),
            pl.BlockSpec(memory_space=pl.ANY),
            pl.BlockSpec((n_exp, two_de), lambda m, plan_ref: (0, 0)),
            pl.BlockSpec(memory_space=pl.ANY),
            pl.BlockSpec((n_exp, d), lambda m, plan_ref: (0, 0)),
        ],
        out_specs=pl.BlockSpec((bm * SUBLANES, LANES), row_map),
        scratch_shapes=[
            pltpu.VMEM((2, d, two_de), jnp.float32),
            pltpu.VMEM((2, d_expert, d), jnp.float32),
            pltpu.SemaphoreType.DMA((2, 2)),
        ],
    )
    return pl.pallas_call(
        functools.partial(_expert_kernel, d_expert),
        grid_spec=grid_spec,
        out_shape=jax.ShapeDtypeStruct((n_blocks * bm * SUBLANES, LANES), jnp.float32),
        compiler_params=pltpu.CompilerParams(
            dimension_semantics=("arbitrary",),
            vmem_limit_bytes=VMEM_LIMIT_BYTES),
        name="expert_ffn",
    )(plan, xs, w_gu, b_gu, w_dn, b_dn)


def _combine_kernel(alpha, slot8_ref, slot8_next_ref, ys_ref, x1t_ref, gates_ref, ln_g_ref, ln_b_ref,
                    out_ref, buf_a, buf_b, sem):
    step = pl.program_id(0)
    last = pl.num_programs(0) - 1
    tt = out_ref.shape[0] // 2

    def gather(slot_ref, half, buf, s):
        def row_copy(j, k):
            return pltpu.make_async_copy(_row_tile(ys_ref, slot_ref[k, half * tt + j]),
                                         _row_tile(buf, (k * tt + j) * SUBLANES), sem.at[s])
        return row_copy

    def reduce_tile(half, buf):
        rows = pl.ds(half * tt, tt)
        gates = gates_ref[rows, :]
        z = alpha * _load_rows(x1t_ref, half * tt, tt)
        for k in range(TOP_K):
            z = z + gates[:, k:k + 1] * _load_rows(buf, k * tt, tt)
        out_ref[rows, :] = _layer_norm(z, ln_g_ref[...], ln_b_ref[...])

    first_a = gather(slot8_ref, 0, buf_a, 0)
    second_b = gather(slot8_ref, 1, buf_b, 1)
    next_a = gather(slot8_next_ref, 0, buf_a, 0)

    @pl.when(step == 0)
    def _():
        _start_rows(first_a, tt)

    _start_rows(second_b, tt)
    _wait_rows(first_a, tt)
    reduce_tile(0, buf_a)

    @pl.when(step < last)
    def _():
        _start_rows(next_a, tt)

    _wait_rows(second_b, tt)
    reduce_tile(1, buf_b)


def _combine_call(slot8, ys, x1t, gates, ln_g, ln_b, alpha):
    n_tok = gates.shape[0]
    d = ln_g.shape[1]
    tt = TOKEN_TILE
    n_steps = n_tok // (2 * tt)
    return pl.pallas_call(
        functools.partial(_combine_kernel, alpha),
        grid=(n_steps,),
        in_specs=[
            pl.BlockSpec((TOP_K, 2 * tt), lambda i: (0, i), memory_space=pltpu.SMEM),
            pl.BlockSpec((TOP_K, 2 * tt), lambda i: (0, jnp.minimum(i + 1, n_steps - 1)),
                         memory_space=pltpu.SMEM),
            pl.BlockSpec(memory_space=pl.ANY),
            pl.BlockSpec((2 * tt * SUBLANES, LANES), lambda i: (i, 0)),
            pl.BlockSpec((2 * tt, LANES), lambda i: (i, 0)),
            pl.BlockSpec((1, d), lambda i: (0, 0)),
            pl.BlockSpec((1, d), lambda i: (0, 0)),
        ],
        out_specs=pl.BlockSpec((2 * tt, d), lambda i: (i, 0)),
        out_shape=jax.ShapeDtypeStruct((n_tok, d), jnp.float32),
        scratch_shapes=[
            pltpu.VMEM((TOP_K * tt * SUBLANES, LANES), jnp.float32),
            pltpu.VMEM((TOP_K * tt * SUBLANES, LANES), jnp.float32),
            pltpu.SemaphoreType.DMA((2,)),
        ],
        compiler_params=pltpu.CompilerParams(
            dimension_semantics=("arbitrary",),
            vmem_limit_bytes=VMEM_LIMIT_BYTES),
        name="combine_ln2",
    )(slot8, slot8, ys, x1t, gates, ln_g, ln_b)


def _block_diag_gates(w_r, w_i):
    n_heads, dh, _ = w_r.shape
    per = GATE_GROUP // dh
    groups = []
    for gi in range(n_heads // per):
        def bd(w):
            m = jnp.zeros((GATE_GROUP, GATE_GROUP), w.dtype)
            for j in range(per):
                m = lax.dynamic_update_slice(m, w[gi * per + j], (j * dh, j * dh))
            return m
        groups.append(jnp.concatenate([bd(w_r), bd(w_i)], axis=1))
    return jnp.stack(groups)


def _layer(x, w_in, conv_w, lconv_w, lconv_b, w_r, b_r, w_i, b_i, lam, w_out, ln1_g, ln1_b,
           w_rt, b_rt, w_gu, b_gu, w_dn, b_dn, ln2_g, ln2_b, alpha):
    bsz, seq, d = x.shape
    assert d == SUBLANES * LANES, "tile-per-row layout needs D == 1024"
    n_tok = bsz * seq
    n_assign = n_tok * TOP_K
    n_blocks = -(-n_assign // ROW_BLOCK) + N_EXPERTS
    expert_cap = n_tok + ROW_BLOCK
    assert expert_cap % ROW_BLOCK == 0
    cap_blocks = expert_cap // ROW_BLOCK
    bf16 = jnp.bfloat16
    row = lambda v: v.reshape(1, -1)

    w_rt_p = jnp.zeros((d, LANES), jnp.float32).at[:, :N_EXPERTS].set(w_rt)
    w_rt_hi = w_rt_p.astype(bf16)
    w_rt_lo = (w_rt_p - w_rt_hi.astype(jnp.float32)).astype(bf16)
    w_rt_p = jnp.concatenate([w_rt_hi, w_rt_lo], axis=1)
    b_rt_p = jnp.zeros((1, LANES), jnp.float32).at[0, :N_EXPERTS].set(b_rt)
    route, gates, counts, x1t, xs = _mixer_call(
        x, w_in.astype(bf16), conv_w, lconv_w, row(lconv_b), _block_diag_gates(w_r, w_i).astype(bf16),
        row(b_r), row(b_i), row(lam), w_out.astype(bf16), row(ln1_g), row(ln1_b), w_rt_p, b_rt_p, alpha,
        expert_cap)

    experts = jnp.arange(N_EXPERTS, dtype=jnp.int32)
    counts = counts[0, :N_EXPERTS]
    padded = (counts + ROW_BLOCK - 1) // ROW_BLOCK * ROW_BLOCK
    pad_end = jnp.cumsum(padded)
    pad_start = pad_end - padded
    top_e = route[:TOP_K, :]
    rank = route[TOP_K:2 * TOP_K, :]
    onehot = top_e[:, :, None] == experts
    slot8 = ((jnp.sum(jnp.where(onehot, pad_start, 0), axis=-1) + rank) * SUBLANES).astype(jnp.int32)
    block_row0 = jnp.arange(n_blocks, dtype=jnp.int32) * ROW_BLOCK
    block_e = jnp.minimum(
        jnp.sum((pad_end[None, :] <= block_row0[:, None]).astype(jnp.int32), axis=1), N_EXPERTS - 1)
    n_used = (pad_end[-1:] // ROW_BLOCK).astype(jnp.int32)

    nonempty = counts > 0
    later_nonempty = jnp.where(nonempty[None, :] & (experts[None, :] > experts[:, None]),
                               experts[None, :], N_EXPERTS)
    next_tbl = jnp.min(later_nonempty, axis=1)
    next_tbl = jnp.where(next_tbl == N_EXPERTS, -1, next_tbl)
    buf_tbl = (jnp.cumsum(nonempty.astype(jnp.int32)) - 1) % 2
    is_e = block_e[:, None] == experts[None, :]
    look = lambda tbl: jnp.sum(jnp.where(is_e, tbl[None, :], 0), axis=1)
    first = (block_row0 == look(pad_start)).astype(jnp.int32)
    last_row = jnp.zeros((n_blocks,), jnp.int32).at[0].set(n_used[0])
    xs_block = block_e * cap_blocks + (block_row0 - look(pad_start)) // ROW_BLOCK
    plan = jnp.stack([block_e, first, look(next_tbl), look(buf_tbl), last_row, xs_block]).astype(jnp.int32)

    ys = _expert_call(plan, xs, w_gu, b_gu, w_dn, b_dn, n_blocks)
    out = _combine_call(slot8, ys, x1t, gates, row(ln2_g), row(ln2_b), alpha)
    return out.reshape(bsz, seq, d)


def kernel(x, w_in, conv_w, lru_conv_w, lru_conv_b, w_rgate, b_rgate, w_igate, b_igate, lru_lambda,
           w_out, ln1_g, ln1_b, w_router, b_router, w_gate_up, b_gate_up, w_down, b_down, ln2_g, ln2_b):
    depth = w_in.shape[0]
    alpha = (2.0 * depth) ** 0.25
    for l in range(depth):
        x = _layer(x, w_in[l], conv_w[l], lru_conv_w[l], lru_conv_b[l], w_rgate[l], b_rgate[l],
                   w_igate[l], b_igate[l], lru_lambda[l], w_out[l], ln1_g[l], ln1_b[l],
                   w_router[l], b_router[l], w_gate_up[l], b_gate_up[l], w_down[l], b_down[l],
                   ln2_g[l], ln2_b[l], alpha)
    return x
```

```python
import functools

import jax
import jax.numpy as jnp
from jax import lax
from jax.experimental import pallas as pl
from jax.experimental.pallas import tpu as pltpu

SHORT_CONV_WIDTH = 3
LRU_CONV_WIDTH = 4
LRU_C = 8.0
N_EXPERTS = 32
TOP_K = 4
SWIGLU_LIMIT = 7.0
SWIGLU_ALPHA = 1.702
LN_EPS = 1e-5

LANES = 128
SUBLANES = 8
MXU_DIM = 256
VMEM_LIMIT_BYTES = 56 * 1024 * 1024

SEQ_TILE = 512
TOKEN_TILE = 256
ROW_BLOCK = 512
GATE_GROUP = MXU_DIM
ISSUE_UNROLL = 8


def _layer_norm(z, g, b):
    mu = jnp.mean(z, axis=-1, keepdims=True)
    zc = z - mu
    var = jnp.mean(zc * zc, axis=-1, keepdims=True)
    return zc * lax.rsqrt(var + LN_EPS) * g + b


def _load_rows(ref, base, n):
    return jnp.concatenate(
        [ref[pl.ds(base * SUBLANES + s, n, stride=SUBLANES), :] for s in range(SUBLANES)], axis=-1)


def _store_rows(ref, val):
    n = val.shape[0]
    for s in range(SUBLANES):
        ref[pl.ds(s, n, stride=SUBLANES), :] = val[:, s * LANES:(s + 1) * LANES]


def _row_tile(ref, row8):
    return ref.at[pl.ds(pl.multiple_of(row8, SUBLANES), SUBLANES), :]


def _start_rows(row_copy, n_tok):
    def body(g, c):
        for jj in range(ISSUE_UNROLL):
            for k in range(TOP_K):
                row_copy(g * ISSUE_UNROLL + jj, k).start(priority=k % 2)
        return c
    lax.fori_loop(0, n_tok // ISSUE_UNROLL, body, 0)


def _wait_rows(row_copy, n_tok):
    def body(g, c):
        for jj in range(ISSUE_UNROLL):
            for k in range(TOP_K):
                row_copy(g * ISSUE_UNROLL + jj, k).wait()
        return c
    lax.fori_loop(0, n_tok // ISSUE_UNROLL, body, 0)


def _mixer_kernel(alpha, d_conv, d_lru,
                  x_ref, w_in_ref, conv_w_ref, lconv_w_ref, lconv_b_ref, w_gate_ref,
                  b_r_ref, b_i_ref, lam_ref, w_out_ref, ln_g_ref, ln_b_ref, w_rt_ref, b_rt_ref,
                  route_ref, gates_ref, counts_ref, x1t_hbm, xs_hbm,
                  g_buf, rx_buf, h_carry, cnt_carry, tri_buf, a_buf, b_buf, hs_buf,
                  stage, slot_vmem, slot_smem, cnt_vmem, cnt_smem, zero_buf,
                  row_sem, x1_sem, smem_sem, zero_sem):
    ts = x_ref.shape[0]
    hdr = SUBLANES
    first_of_seq = pl.program_id(1) == 0
    first_step = jnp.logical_and(pl.program_id(0) == 0, first_of_seq)
    step = pl.program_id(0) * pl.num_programs(1) + pl.program_id(1)
    last_step = step == pl.num_programs(0) * pl.num_programs(1) - 1
    par = step % 2
    cap = xs_hbm.shape[0] // (SUBLANES * N_EXPERTS)

    @pl.when(first_of_seq)
    def _():
        g_buf[0:hdr, :] = jnp.zeros((hdr, d_conv), jnp.float32)
        rx_buf[0:hdr, :] = jnp.zeros((hdr, d_lru), jnp.float32)
        h_carry[...] = jnp.zeros_like(h_carry)

    @pl.when(first_step)
    def _():
        cnt_carry[...] = jnp.zeros_like(cnt_carry)
        r = lax.broadcasted_iota(jnp.int32, (ts, ts), 0)
        c = lax.broadcasted_iota(jnp.int32, (ts, ts), 1)
        tri_buf[...] = (c < r).astype(jnp.bfloat16)

    x = x_ref[...]
    xb = x.astype(jnp.bfloat16)

    def proj(lo, hi):
        return jnp.dot(xb, w_in_ref[:, lo:hi], preferred_element_type=jnp.float32)

    c_b = proj(0, d_conv)
    g = proj(d_conv, 2 * d_conv) * proj(2 * d_conv, 3 * d_conv)
    g_buf[hdr:hdr + ts, :] = g
    y_conv = conv_w_ref[SHORT_CONV_WIDTH - 1:SHORT_CONV_WIDTH, :] * g
    for k in range(SHORT_CONV_WIDTH - 1):
        back = SHORT_CONV_WIDTH - 1 - k
        y_conv = y_conv + conv_w_ref[k:k + 1, :] * g_buf[hdr - back:hdr - back + ts, :]
    y_conv = c_b * y_conv
    g_buf[0:hdr, :] = g_buf[ts:ts + hdr, :]

    off = 3 * d_conv
    r_x = proj(off, off + d_lru)
    rx_buf[hdr:hdr + ts, :] = r_x
    u = lconv_w_ref[LRU_CONV_WIDTH - 1:LRU_CONV_WIDTH, :] * r_x + lconv_b_ref[...]
    for k in range(LRU_CONV_WIDTH - 1):
        back = LRU_CONV_WIDTH - 1 - k
        u = u + lconv_w_ref[k:k + 1, :] * rx_buf[hdr - back:hdr - back + ts, :]
    rx_buf[0:hdr, :] = rx_buf[ts:ts + hdr, :]

    ub = u.astype(jnp.bfloat16)
    r_parts, i_parts = [], []
    for gi in range(d_lru // GATE_GROUP):
        ri = jnp.dot(ub[:, gi * GATE_GROUP:(gi + 1) * GATE_GROUP], w_gate_ref[gi],
                     preferred_element_type=jnp.float32)
        r_parts.append(ri[:, :GATE_GROUP])
        i_parts.append(ri[:, GATE_GROUP:])
    r = jax.nn.sigmoid(jnp.concatenate(r_parts, axis=-1) + b_r_ref[...])
    i = jax.nn.sigmoid(jnp.concatenate(i_parts, axis=-1) + b_i_ref[...])

    neg_lam = -lam_ref[...]
    softplus = jnp.maximum(neg_lam, 0.0) + jnp.log1p(jnp.exp(-jnp.abs(neg_lam)))
    log_a = (-LRU_C) * r * softplus
    a = jnp.exp(log_a)
    bb = jnp.sqrt(jnp.tanh(-log_a) * (a * a + 1.0)) * (i * u)

    n_grp = ts // SUBLANES
    n_col = d_lru // LANES
    for c in range(n_col):
        a_buf[c * ts:(c + 1) * ts, :] = a[:, c * LANES:(c + 1) * LANES]
        b_buf[c * ts:(c + 1) * ts, :] = bb[:, c * LANES:(c + 1) * LANES]

    def slab(buf, j):
        return jnp.concatenate(
            [buf[pl.ds(c * ts + j, n_grp, stride=SUBLANES), :] for c in range(n_col)], axis=-1)

    cum_a, cum_b = [slab(a_buf, 0)], [slab(b_buf, 0)]
    for j in range(1, SUBLANES):
        a_j, b_j = slab(a_buf, j), slab(b_buf, j)
        cum_b.append(a_j * cum_b[-1] + b_j)
        cum_a.append(a_j * cum_a[-1])

    grp_row = lax.broadcasted_iota(jnp.int32, (n_grp, d_lru), 0)
    grp_a, grp_b = cum_a[-1], cum_b[-1]
    d = 1
    while d < n_grp:
        keep = grp_row >= d
        a_sh = jnp.where(keep, pltpu.roll(grp_a, d, 0), 1.0)
        b_sh = jnp.where(keep, pltpu.roll(grp_b, d, 0), 0.0)
        grp_b = grp_a * b_sh + grp_b
        grp_a = grp_a * a_sh
        d *= 2
    grp_end = grp_a * h_carry[...] + grp_b
    grp_start = jnp.where(grp_row == 0, h_carry[...], pltpu.roll(grp_end, 1, 0))
    h_carry[...] = grp_end[n_grp - 1:n_grp, :]

    for j in range(SUBLANES):
        h_j = cum_a[j] * grp_start + cum_b[j]
        for c in range(n_col):
            hs_buf[pl.ds(c * ts + j, n_grp, stride=SUBLANES), :] = h_j[:, c * LANES:(c + 1) * LANES]
    h = jnp.concatenate([hs_buf[c * ts:(c + 1) * ts, :] for c in range(n_col)], axis=-1)

    r_g = proj(off + d_lru, off + 2 * d_lru)
    y_lru = h * jax.nn.gelu(r_g)

    mixed = (jnp.dot(y_conv.astype(jnp.bfloat16), w_out_ref[0:d_conv, :],
                     preferred_element_type=jnp.float32)
             + jnp.dot(y_lru.astype(jnp.bfloat16), w_out_ref[d_conv:d_conv + d_lru, :],
                       preferred_element_type=jnp.float32))
    x1 = _layer_norm(alpha * x + mixed, ln_g_ref[...], ln_b_ref[...])
    _store_rows(stage.at[par], x1)

    x1_hi = x1.astype(jnp.bfloat16)
    x1_lo = (x1 - x1_hi.astype(jnp.float32)).astype(jnp.bfloat16)
    hi_part = jnp.dot(x1_hi, w_rt_ref[...], preferred_element_type=jnp.float32)
    lo_part = jnp.dot(x1_lo, w_rt_ref[:, 0:LANES], preferred_element_type=jnp.float32)
    logits = hi_part[:, 0:LANES] + hi_part[:, LANES:2 * LANES] + lo_part + b_rt_ref[...]
    lane = lax.broadcasted_iota(jnp.int32, (ts, LANES), 1)
    lane_f = lane.astype(jnp.float32)
    neg_inf = jnp.float32(-jnp.inf)
    work = jnp.where(lane < N_EXPERTS, logits, neg_inf)
    sel = jnp.zeros((ts, LANES), jnp.bool_)
    top_v, top_e = [], []
    for _ in range(TOP_K):
        m = jnp.max(work, axis=-1, keepdims=True)
        e_f = jnp.min(jnp.where(work == m, lane_f, float(LANES)), axis=-1, keepdims=True)
        hit = lane_f == e_f
        sel = jnp.logical_or(sel, hit)
        work = jnp.where(hit, neg_inf, work)
        top_v.append(m)
        top_e.append(e_f)
    exps = [jnp.exp(v - top_v[0]) for v in top_v]
    denom = exps[0]
    for ex in exps[1:]:
        denom = denom + ex

    sel_f = sel.astype(jnp.float32)
    before = jnp.dot(tri_buf[...], sel_f.astype(jnp.bfloat16), preferred_element_type=jnp.float32)
    pos = before + cnt_carry[...]
    cnt_carry[...] = cnt_carry[...] + jnp.sum(sel_f, axis=0, keepdims=True)
    counts = cnt_carry[...].astype(jnp.int32)
    counts_ref[...] = counts

    route = jnp.zeros((ts, LANES), jnp.float32)
    gates = jnp.zeros((ts, LANES), jnp.float32)
    for k in range(TOP_K):
        rank = jnp.sum(jnp.where(lane_f == top_e[k], pos, 0.0), axis=-1, keepdims=True)
        route = jnp.where(lane == k, top_e[k], route)
        route = jnp.where(lane == TOP_K + k, rank, route)
        gates = jnp.where(lane == k, exps[k] / denom, gates)
    gates_ref[...] = gates
    route_t = jnp.transpose(route)[0:2 * TOP_K, :].astype(jnp.int32)
    route_ref[...] = route_t

    slots8 = (route_t[0:TOP_K, :] * cap + route_t[TOP_K:2 * TOP_K, :]) * SUBLANES
    slot_vmem[...] = jnp.concatenate([slots8, jnp.zeros_like(slots8)], axis=0)
    def rows_of(p):
        def row_copy(j, k):
            return pltpu.make_async_copy(_row_tile(stage.at[p], j * SUBLANES),
                                         _row_tile(xs_hbm, slot_smem[p * SUBLANES + k, j]), row_sem.at[p])
        return row_copy

    def x1_copy(p, s):
        rows = ts * SUBLANES
        return pltpu.make_async_copy(stage.at[p], x1t_hbm.at[pl.ds(pl.multiple_of(s * rows, rows), rows), :],
                                     x1_sem.at[p])

    def dispatch_from(p):
        to_smem = pltpu.make_async_copy(
            slot_vmem, slot_smem.at[pl.ds(p * SUBLANES, SUBLANES), :], smem_sem)
        to_smem.start()

        @pl.when(step > 0)
        def _():
            _wait_rows(rows_of(1 - p), ts)
            x1_copy(1 - p, step - 1).wait()

        to_smem.wait()
        _start_rows(rows_of(p), ts)
        x1_copy(p, step).start()

        @pl.when(last_step)
        def _():
            _wait_rows(rows_of(p), ts)
            x1_copy(p, step).wait()

    for p in range(2):
        pl.when(par == p)(functools.partial(dispatch_from, p))

    @pl.when(last_step)
    def _():
        zero_buf[...] = jnp.zeros_like(zero_buf)
        cnt_vmem[...] = jnp.broadcast_to(counts, cnt_vmem.shape)
        cnt_copy = pltpu.make_async_copy(cnt_vmem, cnt_smem, smem_sem)
        cnt_copy.start()
        cnt_copy.wait()

        def zcopy(e):
            start = pl.multiple_of((e * cap + cnt_smem[0, e]) * SUBLANES, SUBLANES)
            return pltpu.make_async_copy(zero_buf, xs_hbm.at[pl.ds(start, zero_buf.shape[0]), :], zero_sem)

        def zstart(e, c):
            zcopy(e).start()
            return c

        def zwait(e, c):
            zcopy(e).wait()
            return c

        lax.fori_loop(0, N_EXPERTS, zstart, 0)
        lax.fori_loop(0, N_EXPERTS, zwait, 0)


def _mixer_call(x, w_in, conv_w, lconv_w, lconv_b, w_gate, b_r, b_i, lam, w_out, ln_g, ln_b,
                w_rt, b_rt, alpha, expert_cap):
    bsz, seq, d = x.shape
    d_conv = conv_w.shape[1]
    d_lru = lconv_w.shape[1]
    ts = SEQ_TILE
    n_tok = bsz * seq
    nt = seq // ts

    def const(shape):
        return pl.BlockSpec(shape, lambda b, t: (0,) * len(shape))

    tok_map = lambda b, t: (b * nt + t, 0)
    kernel = functools.partial(_mixer_kernel, alpha, d_conv, d_lru)
    return pl.pallas_call(
        kernel,
        grid=(bsz, nt),
        in_specs=[
            pl.BlockSpec((None, ts, d), lambda b, t: (b, t, 0)),
            const(w_in.shape), const(conv_w.shape), const(lconv_w.shape), const(lconv_b.shape),
            const(w_gate.shape), const(b_r.shape), const(b_i.shape), const(lam.shape),
            const(w_out.shape), const(ln_g.shape), const(ln_b.shape), const(w_rt.shape),
            const(b_rt.shape),
        ],
        out_specs=[
            pl.BlockSpec((2 * TOP_K, ts), lambda b, t: (0, b * nt + t)),
            pl.BlockSpec((ts, LANES), tok_map),
            pl.BlockSpec((1, LANES), lambda b, t: (0, 0)),
            pl.BlockSpec(memory_space=pl.ANY),
            pl.BlockSpec(memory_space=pl.ANY),
        ],
        out_shape=[
            jax.ShapeDtypeStruct((2 * TOP_K, n_tok), jnp.int32),
            jax.ShapeDtypeStruct((n_tok, LANES), jnp.float32),
            jax.ShapeDtypeStruct((1, LANES), jnp.int32),
            jax.ShapeDtypeStruct((n_tok * SUBLANES, LANES), jnp.float32),
            jax.ShapeDtypeStruct((N_EXPERTS * expert_cap * SUBLANES, LANES), jnp.float32),
        ],
        scratch_shapes=[
            pltpu.VMEM((ts + SUBLANES, d_conv), jnp.float32),
            pltpu.VMEM((ts + SUBLANES, d_lru), jnp.float32),
            pltpu.VMEM((1, d_lru), jnp.float32),
            pltpu.VMEM((1, LANES), jnp.float32),
            pltpu.VMEM((ts, ts), jnp.bfloat16),
            pltpu.VMEM((ts * d_lru // LANES, LANES), jnp.float32),
            pltpu.VMEM((ts * d_lru // LANES, LANES), jnp.float32),
            pltpu.VMEM((ts * d_lru // LANES, LANES), jnp.float32),
            pltpu.VMEM((2, ts * SUBLANES, LANES), jnp.float32),
            pltpu.VMEM((SUBLANES, ts), jnp.int32),
            pltpu.SMEM((2 * SUBLANES, ts), jnp.int32),
            pltpu.VMEM((SUBLANES, LANES), jnp.int32),
            pltpu.SMEM((SUBLANES, LANES), jnp.int32),
            pltpu.VMEM((ROW_BLOCK * SUBLANES, LANES), jnp.float32),
            pltpu.SemaphoreType.DMA((2,)),
            pltpu.SemaphoreType.DMA((2,)),
            pltpu.SemaphoreType.DMA,
            pltpu.SemaphoreType.DMA,
        ],
        compiler_params=pltpu.CompilerParams(
            dimension_semantics=("arbitrary", "arbitrary"),
            vmem_limit_bytes=VMEM_LIMIT_BYTES),
        name="mixer_ln1_router",
    )(x, w_in, conv_w, lconv_w, lconv_b, w_gate, b_r, b_i, lam, w_out, ln_g, ln_b, w_rt, b_rt)


def _expert_kernel(d_expert, plan_ref, xs_ref, w_gu_hbm, b_gu_ref, w_dn_hbm, b_dn_ref, ys_ref,
                   wgu_buf, wdn_buf, sem):
    m = pl.program_id(0)
    bm = xs_ref.shape[0] // SUBLANES

    def fetch(e, s):
        return (pltpu.make_async_copy(w_gu_hbm.at[e], wgu_buf.at[s], sem.at[0, s]),
                pltpu.make_async_copy(w_dn_hbm.at[e], wdn_buf.at[s], sem.at[1, s]))

    @pl.when(m < plan_ref[4, 0])
    def _():
        e = plan_ref[0, m]
        s = plan_ref[3, m]

        @pl.when(m == 0)
        def _():
            for c in fetch(e, s):
                c.start()

        @pl.when(plan_ref[1, m] == 1)
        def _():
            nxt = plan_ref[2, m]

            @pl.when(nxt >= 0)
            def _():
                for c in fetch(nxt, 1 - s):
                    c.start()

            for c in fetch(e, s):
                c.wait()

        xb = _load_rows(xs_ref, 0, bm)
        h = jnp.dot(xb, wgu_buf[s], preferred_element_type=jnp.float32) + b_gu_ref[pl.ds(e, 1), :]
        h_glu = jnp.minimum(h[:, :d_expert], SWIGLU_LIMIT)
        h_lin = jnp.clip(h[:, d_expert:], -SWIGLU_LIMIT, SWIGLU_LIMIT)
        act = h_glu * jax.nn.sigmoid(SWIGLU_ALPHA * h_glu) * (h_lin + 1.0)
        y = jnp.dot(act, wdn_buf[s], preferred_element_type=jnp.float32) + b_dn_ref[pl.ds(e, 1), :]
        _store_rows(ys_ref, y)


def _expert_call(plan, xs, w_gu, b_gu, w_dn, b_dn, n_blocks):
    n_exp, d, two_de = w_gu.shape
    d_expert = w_dn.shape[1]
    bm = ROW_BLOCK

    def row_map(m, plan_ref):
        return (jnp.minimum(m, plan_ref[4, 0] - 1), 0)

    def xs_map(m, plan_ref):
        return (plan_ref[5, jnp.minimum(m, plan_ref[4, 0] - 1)], 0)

    grid_spec = pltpu.PrefetchScalarGridSpec(
        num_scalar_prefetch=1,
        grid=(n_blocks,),
        in_specs=[
            pl.BlockSpec((bm * SUBLANES, LANES), xs_map),
            pl.BlockSpec(memory_space=pl.ANY),
            pl.BlockSpec((n_exp, two_de), lambda m, plan_ref: (0, 0)),
            pl.BlockSpec(memory_space=pl.ANY),
            pl.BlockSpec((n_exp, d), lambda m, plan_ref: (0, 0)),
        ],
        out_specs=pl.BlockSpec((bm * SUBLANES, LANES), row_map),
        scratch_shapes=[
            pltpu.VMEM((2, d, two_de), jnp.float32),
            pltpu.VMEM((2, d_expert, d), jnp.float32),
            pltpu.SemaphoreType.DMA((2, 2)),
        ],
    )
    return pl.pallas_call(
        functools.partial(_expert_kernel, d_expert),
        grid_spec=grid_spec,
        out_shape=jax.ShapeDtypeStruct((n_blocks * bm * SUBLANES, LANES), jnp.float32),
        compiler_params=pltpu.CompilerParams(
            dimension_semantics=("arbitrary",),
            vmem_limit_bytes=VMEM_LIMIT_BYTES),
        name="expert_ffn",
    )(plan, xs, w_gu, b_gu, w_dn, b_dn)


def _combine_kernel(alpha, slot8_ref, slot8_next_ref, ys_ref, x1t_ref, gates_ref, ln_g_ref, ln_b_ref,
                    out_ref, buf_a, buf_b, sem):
    step = pl.program_id(0)
    last = pl.num_programs(0) - 1
    tt = out_ref.shape[0] // 2

    def gather(slot_ref, half, buf, s):
        def row_copy(j, k):
            return pltpu.make_async_copy(_row_tile(ys_ref, slot_ref[(half * tt + j) * TOP_K + k]),
                                         _row_tile(buf, (k * tt + j) * SUBLANES), sem.at[s])
        return row_copy

    def reduce_tile(half, buf):
        rows = pl.ds(half * tt, tt)
        gates = gates_ref[rows, :]
        z = alpha * _load_rows(x1t_ref, half * tt, tt)
        for k in range(TOP_K):
            z = z + gates[:, k:k + 1] * _load_rows(buf, k * tt, tt)
        out_ref[rows, :] = _layer_norm(z, ln_g_ref[...], ln_b_ref[...])

    first_a = gather(slot8_ref, 0, buf_a, 0)
    second_b = gather(slot8_ref, 1, buf_b, 1)
    next_a = gather(slot8_next_ref, 0, buf_a, 0)

    @pl.when(step == 0)
    def _():
        _start_rows(first_a, tt)

    _start_rows(second_b, tt)
    _wait_rows(first_a, tt)
    reduce_tile(0, buf_a)

    @pl.when(step < last)
    def _():
        _start_rows(next_a, tt)

    _wait_rows(second_b, tt)
    reduce_tile(1, buf_b)


def _combine_call(slot8, ys, x1t, gates, ln_g, ln_b, alpha):
    n_tok = gates.shape[0]
    d = ln_g.shape[1]
    tt = TOKEN_TILE
    n_steps = n_tok // (2 * tt)
    return pl.pallas_call(
        functools.partial(_combine_kernel, alpha),
        grid=(n_steps,),
        in_specs=[
            pl.BlockSpec((2 * tt * TOP_K,), lambda i: (i,), memory_space=pltpu.SMEM),
            pl.BlockSpec((2 * tt * TOP_K,), lambda i: (jnp.minimum(i + 1, n_steps - 1),),
                         memory_space=pltpu.SMEM),
            pl.BlockSpec(memory_space=pl.ANY),
            pl.BlockSpec((2 * tt * SUBLANES, LANES), lambda i: (i, 0)),
            pl.BlockSpec((2 * tt, LANES), lambda i: (i, 0)),
            pl.BlockSpec((1, d), lambda i: (0, 0)),
            pl.BlockSpec((1, d), lambda i: (0, 0)),
        ],
        out_specs=pl.BlockSpec((2 * tt, d), lambda i: (i, 0)),
        out_shape=jax.ShapeDtypeStruct((n_tok, d), jnp.float32),
        scratch_shapes=[
            pltpu.VMEM((TOP_K * tt * SUBLANES, LANES), jnp.float32),
            pltpu.VMEM((TOP_K * tt * SUBLANES, LANES), jnp.float32),
            pltpu.SemaphoreType.DMA((2,)),
        ],
        compiler_params=pltpu.CompilerParams(
            dimension_semantics=("arbitrary",),
            vmem_limit_bytes=VMEM_LIMIT_BYTES),
        name="combine_ln2",
    )(slot8, slot8, ys, x1t, gates, ln_g, ln_b)


def _block_diag_gates(w_r, w_i):
    n_heads, dh, _ = w_r.shape
    per = GATE_GROUP // dh
    groups = []
    for gi in range(n_heads // per):
        def bd(w):
            m = jnp.zeros((GATE_GROUP, GATE_GROUP), w.dtype)
            for j in range(per):
                m = lax.dynamic_update_slice(m, w[gi * per + j], (j * dh, j * dh))
            return m
        groups.append(jnp.concatenate([bd(w_r), bd(w_i)], axis=1))
    return jnp.stack(groups)


def _layer(x, w_in, conv_w, lconv_w, lconv_b, w_r, b_r, w_i, b_i, lam, w_out, ln1_g, ln1_b,
           w_rt, b_rt, w_gu, b_gu, w_dn, b_dn, ln2_g, ln2_b, alpha):
    bsz, seq, d = x.shape
    assert d == SUBLANES * LANES, "tile-per-row layout needs D == 1024"
    n_tok = bsz * seq
    n_assign = n_tok * TOP_K
    n_blocks = -(-n_assign // ROW_BLOCK) + N_EXPERTS
    expert_cap = n_tok + ROW_BLOCK
    assert expert_cap % ROW_BLOCK == 0
    cap_blocks = expert_cap // ROW_BLOCK
    bf16 = jnp.bfloat16
    row = lambda v: v.reshape(1, -1)

    w_rt_p = jnp.zeros((d, LANES), jnp.float32).at[:, :N_EXPERTS].set(w_rt)
    w_rt_hi = w_rt_p.astype(bf16)
    w_rt_lo = (w_rt_p - w_rt_hi.astype(jnp.float32)).astype(bf16)
    w_rt_p = jnp.concatenate([w_rt_hi, w_rt_lo], axis=1)
    b_rt_p = jnp.zeros((1, LANES), jnp.float32).at[0, :N_EXPERTS].set(b_rt)
    route, gates, counts, x1t, xs = _mixer_call(
        x, w_in.astype(bf16), conv_w, lconv_w, row(lconv_b), _block_diag_gates(w_r, w_i).astype(bf16),
        row(b_r), row(b_i), row(lam), w_out.astype(bf16), row(ln1_g), row(ln1_b), w_rt_p, b_rt_p, alpha,
        expert_cap)

    experts = jnp.arange(N_EXPERTS, dtype=jnp.int32)
    counts = counts[0, :N_EXPERTS]
    padded = (counts + ROW_BLOCK - 1) // ROW_BLOCK * ROW_BLOCK
    pad_end = jnp.cumsum(padded)
    pad_start = pad_end - padded
    top_e = route[:TOP_K, :]
    rank = route[TOP_K:2 * TOP_K, :]
    onehot = top_e[:, :, None] == experts
    slot8 = ((jnp.sum(jnp.where(onehot, pad_start, 0), axis=-1) + rank) * SUBLANES).astype(jnp.int32)
    slot8 = slot8.T.reshape(-1)
    block_row0 = jnp.arange(n_blocks, dtype=jnp.int32) * ROW_BLOCK
    block_e = jnp.minimum(
        jnp.sum((pad_end[None, :] <= block_row0[:, None]).astype(jnp.int32), axis=1), N_EXPERTS - 1)
    n_used = (pad_end[-1:] // ROW_BLOCK).astype(jnp.int32)

    nonempty = counts > 0
    later_nonempty = jnp.where(nonempty[None, :] & (experts[None, :] > experts[:, None]),
                               experts[None, :], N_EXPERTS)
    next_tbl = jnp.min(later_nonempty, axis=1)
    next_tbl = jnp.where(next_tbl == N_EXPERTS, -1, next_tbl)
    buf_tbl = (jnp.cumsum(nonempty.astype(jnp.int32)) - 1) % 2
    is_e = block_e[:, None] == experts[None, :]
    look = lambda tbl: jnp.sum(jnp.where(is_e, tbl[None, :], 0), axis=1)
    first = (block_row0 == look(pad_start)).astype(jnp.int32)
    last_row = jnp.zeros((n_blocks,), jnp.int32).at[0].set(n_used[0])
    xs_block = block_e * cap_blocks + (block_row0 - look(pad_start)) // ROW_BLOCK
    plan = jnp.stack([block_e, first, look(next_tbl), look(buf_tbl), last_row, xs_block]).astype(jnp.int32)

    ys = _expert_call(plan, xs, w_gu, b_gu, w_dn, b_dn, n_blocks)
    out = _combine_call(slot8, ys, x1t, gates, row(ln2_g), row(ln2_b), alpha)
    return out.reshape(bsz, seq, d)


def kernel(x, w_in, conv_w, lru_conv_w, lru_conv_b, w_rgate, b_rgate, w_igate, b_igate, lru_lambda,
           w_out, ln1_g, ln1_b, w_router, b_router, w_gate_up, b_gate_up, w_down, b_down, ln2_g, ln2_b):
    depth = w_in.shape[0]
    alpha = (2.0 * depth) ** 0.25
    for l in range(depth):
        x = _layer(x, w_in[l], conv_w[l], lru_conv_w[l], lru_conv_b[l], w_rgate[l], b_rgate[l],
                   w_igate[l], b_igate[l], lru_lambda[l], w_out[l], ln1_g[l], ln1_b[l],
                   w_router[l], b_router[l], w_gate_up[l], b_gate_up[l], w_down[l], b_down[l],
                   ln2_g[l], ln2_b[l], alpha)
    return x
```

```python
import functools

import jax
import jax.numpy as jnp
from jax import lax
from jax.experimental import pallas as pl
from jax.experimental.pallas import tpu as pltpu

SHORT_CONV_WIDTH = 3
LRU_CONV_WIDTH = 4
LRU_C = 8.0
N_EXPERTS = 32
TOP_K = 4
SWIGLU_LIMIT = 7.0
SWIGLU_ALPHA = 1.702
LN_EPS = 1e-5

LANES = 128
SUBLANES = 8
MXU_DIM = 256
VMEM_LIMIT_BYTES = 56 * 1024 * 1024

SEQ_TILE = 1024
TOKEN_TILE = 256
ROW_BLOCK = 512
GATE_GROUP = MXU_DIM
ISSUE_UNROLL = 8


def _layer_norm(z, g, b):
    mu = jnp.mean(z, axis=-1, keepdims=True)
    zc = z - mu
    var = jnp.mean(zc * zc, axis=-1, keepdims=True)
    return zc * lax.rsqrt(var + LN_EPS) * g + b


def _load_rows(ref, base, n):
    return jnp.concatenate(
        [ref[pl.ds(base * SUBLANES + s, n, stride=SUBLANES), :] for s in range(SUBLANES)], axis=-1)


def _store_rows(ref, val):
    n = val.shape[0]
    for s in range(SUBLANES):
        ref[pl.ds(s, n, stride=SUBLANES), :] = val[:, s * LANES:(s + 1) * LANES]


def _row_tile(ref, row8):
    return ref.at[pl.ds(pl.multiple_of(row8, SUBLANES), SUBLANES), :]


def _start_rows(row_copy, n_tok):
    def body(g, c):
        for jj in range(ISSUE_UNROLL):
            for k in range(TOP_K):
                row_copy(g * ISSUE_UNROLL + jj, k).start(priority=k % 2)
        return c
    lax.fori_loop(0, n_tok // ISSUE_UNROLL, body, 0)


def _wait_rows(row_copy, n_tok):
    def body(g, c):
        for jj in range(ISSUE_UNROLL):
            for k in range(TOP_K):
                row_copy(g * ISSUE_UNROLL + jj, k).wait()
        return c
    lax.fori_loop(0, n_tok // ISSUE_UNROLL, body, 0)


def _mixer_kernel(alpha, d_conv, d_lru,
                  x_ref, w_in_ref, conv_w_ref, lconv_w_ref, lconv_b_ref, w_gate_ref,
                  b_r_ref, b_i_ref, lam_ref, w_out_ref, ln_g_ref, ln_b_ref, w_rt_ref, b_rt_ref,
                  route_ref, gates_ref, counts_ref, x1t_hbm, xs_hbm,
                  g_buf, rx_buf, h_carry, cnt_carry, tri_buf, a_buf, b_buf, hs_buf,
                  stage, slot_vmem, slot_smem, cnt_vmem, cnt_smem, zero_buf,
                  row_sem, x1_sem, smem_sem, zero_sem):
    ts = x_ref.shape[0]
    hdr = SUBLANES
    first_of_seq = pl.program_id(1) == 0
    first_step = jnp.logical_and(pl.program_id(0) == 0, first_of_seq)
    step = pl.program_id(0) * pl.num_programs(1) + pl.program_id(1)
    last_step = step == pl.num_programs(0) * pl.num_programs(1) - 1
    par = step % 2
    cap = xs_hbm.shape[0] // (SUBLANES * N_EXPERTS)

    @pl.when(first_of_seq)
    def _():
        g_buf[0:hdr, :] = jnp.zeros((hdr, d_conv), jnp.float32)
        rx_buf[0:hdr, :] = jnp.zeros((hdr, d_lru), jnp.float32)
        h_carry[...] = jnp.zeros_like(h_carry)

    @pl.when(first_step)
    def _():
        cnt_carry[...] = jnp.zeros_like(cnt_carry)
        r = lax.broadcasted_iota(jnp.int32, (ts, ts), 0)
        c = lax.broadcasted_iota(jnp.int32, (ts, ts), 1)
        tri_buf[...] = (c < r).astype(jnp.bfloat16)

    x = x_ref[...]
    xb = x.astype(jnp.bfloat16)

    def proj(lo, hi):
        return jnp.dot(xb, w_in_ref[:, lo:hi], preferred_element_type=jnp.float32)

    c_b = proj(0, d_conv)
    g = proj(d_conv, 2 * d_conv) * proj(2 * d_conv, 3 * d_conv)
    g_buf[hdr:hdr + ts, :] = g
    y_conv = conv_w_ref[SHORT_CONV_WIDTH - 1:SHORT_CONV_WIDTH, :] * g
    for k in range(SHORT_CONV_WIDTH - 1):
        back = SHORT_CONV_WIDTH - 1 - k
        y_conv = y_conv + conv_w_ref[k:k + 1, :] * g_buf[hdr - back:hdr - back + ts, :]
    y_conv = c_b * y_conv
    g_buf[0:hdr, :] = g_buf[ts:ts + hdr, :]

    off = 3 * d_conv
    r_x = proj(off, off + d_lru)
    rx_buf[hdr:hdr + ts, :] = r_x
    u = lconv_w_ref[LRU_CONV_WIDTH - 1:LRU_CONV_WIDTH, :] * r_x + lconv_b_ref[...]
    for k in range(LRU_CONV_WIDTH - 1):
        back = LRU_CONV_WIDTH - 1 - k
        u = u + lconv_w_ref[k:k + 1, :] * rx_buf[hdr - back:hdr - back + ts, :]
    rx_buf[0:hdr, :] = rx_buf[ts:ts + hdr, :]

    ub = u.astype(jnp.bfloat16)
    r_parts, i_parts = [], []
    for gi in range(d_lru // GATE_GROUP):
        ri = jnp.dot(ub[:, gi * GATE_GROUP:(gi + 1) * GATE_GROUP], w_gate_ref[gi],
                     preferred_element_type=jnp.float32)
        r_parts.append(ri[:, :GATE_GROUP])
        i_parts.append(ri[:, GATE_GROUP:])
    r = jax.nn.sigmoid(jnp.concatenate(r_parts, axis=-1) + b_r_ref[...])
    i = jax.nn.sigmoid(jnp.concatenate(i_parts, axis=-1) + b_i_ref[...])

    neg_lam = -lam_ref[...]
    softplus = jnp.maximum(neg_lam, 0.0) + jnp.log1p(jnp.exp(-jnp.abs(neg_lam)))
    log_a = (-LRU_C) * r * softplus
    a = jnp.exp(log_a)
    bb = jnp.sqrt(jnp.tanh(-log_a) * (a * a + 1.0)) * (i * u)

    n_grp = ts // SUBLANES
    n_col = d_lru // LANES
    for c in range(n_col):
        a_buf[c * ts:(c + 1) * ts, :] = a[:, c * LANES:(c + 1) * LANES]
        b_buf[c * ts:(c + 1) * ts, :] = bb[:, c * LANES:(c + 1) * LANES]

    def slab(buf, j):
        return jnp.concatenate(
            [buf[pl.ds(c * ts + j, n_grp, stride=SUBLANES), :] for c in range(n_col)], axis=-1)

    cum_a, cum_b = [slab(a_buf, 0)], [slab(b_buf, 0)]
    for j in range(1, SUBLANES):
        a_j, b_j = slab(a_buf, j), slab(b_buf, j)
        cum_b.append(a_j * cum_b[-1] + b_j)
        cum_a.append(a_j * cum_a[-1])

    grp_row = lax.broadcasted_iota(jnp.int32, (n_grp, d_lru), 0)
    grp_a, grp_b = cum_a[-1], cum_b[-1]
    d = 1
    while d < n_grp:
        keep = grp_row >= d
        a_sh = jnp.where(keep, pltpu.roll(grp_a, d, 0), 1.0)
        b_sh = jnp.where(keep, pltpu.roll(grp_b, d, 0), 0.0)
        grp_b = grp_a * b_sh + grp_b
        grp_a = grp_a * a_sh
        d *= 2
    grp_end = grp_a * h_carry[...] + grp_b
    grp_start = jnp.where(grp_row == 0, h_carry[...], pltpu.roll(grp_end, 1, 0))
    h_carry[...] = grp_end[n_grp - 1:n_grp, :]

    for j in range(SUBLANES):
        h_j = cum_a[j] * grp_start + cum_b[j]
        for c in range(n_col):
            hs_buf[pl.ds(c * ts + j, n_grp, stride=SUBLANES), :] = h_j[:, c * LANES:(c + 1) * LANES]
    h = jnp.concatenate([hs_buf[c * ts:(c + 1) * ts, :] for c in range(n_col)], axis=-1)

    r_g = proj(off + d_lru, off + 2 * d_lru)
    y_lru = h * jax.nn.gelu(r_g)

    mixed = (jnp.dot(y_conv.astype(jnp.bfloat16), w_out_ref[0:d_conv, :],
                     preferred_element_type=jnp.float32)
             + jnp.dot(y_lru.astype(jnp.bfloat16), w_out_ref[d_conv:d_conv + d_lru, :],
                       preferred_element_type=jnp.float32))
    x1 = _layer_norm(alpha * x + mixed, ln_g_ref[...], ln_b_ref[...])
    _store_rows(stage.at[par], x1)

    x1_hi = x1.astype(jnp.bfloat16)
    x1_lo = (x1 - x1_hi.astype(jnp.float32)).astype(jnp.bfloat16)
    hi_part = jnp.dot(x1_hi, w_rt_ref[...], preferred_element_type=jnp.float32)
    lo_part = jnp.dot(x1_lo, w_rt_ref[:, 0:LANES], preferred_element_type=jnp.float32)
    logits = hi_part[:, 0:LANES] + hi_part[:, LANES:2 * LANES] + lo_part + b_rt_ref[...]
    lane = lax.broadcasted_iota(jnp.int32, (ts, LANES), 1)
    lane_f = lane.astype(jnp.float32)
    neg_inf = jnp.float32(-jnp.inf)
    work = jnp.where(lane < N_EXPERTS, logits, neg_inf)
    sel = jnp.zeros((ts, LANES), jnp.bool_)
    top_v, top_e = [], []
    for _ in range(TOP_K):
        m = jnp.max(work, axis=-1, keepdims=True)
        e_f = jnp.min(jnp.where(work == m, lane_f, float(LANES)), axis=-1, keepdims=True)
        hit = lane_f == e_f
        sel = jnp.logical_or(sel, hit)
        work = jnp.where(hit, neg_inf, work)
        top_v.append(m)
        top_e.append(e_f)
    exps = [jnp.exp(v - top_v[0]) for v in top_v]
    denom = exps[0]
    for ex in exps[1:]:
        denom = denom + ex

    sel_f = sel.astype(jnp.float32)
    before = jnp.dot(tri_buf[...], sel_f.astype(jnp.bfloat16), preferred_element_type=jnp.float32)
    pos = before + cnt_carry[...]
    cnt_carry[...] = cnt_carry[...] + jnp.sum(sel_f, axis=0, keepdims=True)
    counts = cnt_carry[...].astype(jnp.int32)
    counts_ref[...] = counts

    route = jnp.zeros((ts, LANES), jnp.float32)
    gates = jnp.zeros((ts, LANES), jnp.float32)
    for k in range(TOP_K):
        rank = jnp.sum(jnp.where(lane_f == top_e[k], pos, 0.0), axis=-1, keepdims=True)
        route = jnp.where(lane == k, top_e[k], route)
        route = jnp.where(lane == TOP_K + k, rank, route)
        gates = jnp.where(lane == k, exps[k] / denom, gates)
    gates_ref[...] = gates
    route_t = jnp.transpose(route)[0:2 * TOP_K, :].astype(jnp.int32)
    route_ref[...] = route_t

    slots8 = (route_t[0:TOP_K, :] * cap + route_t[TOP_K:2 * TOP_K, :]) * SUBLANES
    slot_vmem[...] = jnp.concatenate([slots8, jnp.zeros_like(slots8)], axis=0)
    def rows_of(p):
        def row_copy(j, k):
            return pltpu.make_async_copy(_row_tile(stage.at[p], j * SUBLANES),
                                         _row_tile(xs_hbm, slot_smem[p * SUBLANES + k, j]), row_sem.at[p])
        return row_copy

    def x1_copy(p, s):
        rows = ts * SUBLANES
        return pltpu.make_async_copy(stage.at[p], x1t_hbm.at[pl.ds(pl.multiple_of(s * rows, rows), rows), :],
                                     x1_sem.at[p])

    def dispatch_from(p):
        to_smem = pltpu.make_async_copy(
            slot_vmem, slot_smem.at[pl.ds(p * SUBLANES, SUBLANES), :], smem_sem)
        to_smem.start()

        @pl.when(step > 0)
        def _():
            _wait_rows(rows_of(1 - p), ts)
            x1_copy(1 - p, step - 1).wait()

        to_smem.wait()
        _start_rows(rows_of(p), ts)
        x1_copy(p, step).start()

        @pl.when(last_step)
        def _():
            _wait_rows(rows_of(p), ts)
            x1_copy(p, step).wait()

    for p in range(2):
        pl.when(par == p)(functools.partial(dispatch_from, p))

    @pl.when(last_step)
    def _():
        zero_buf[...] = jnp.zeros_like(zero_buf)
        cnt_vmem[...] = jnp.broadcast_to(counts, cnt_vmem.shape)
        cnt_copy = pltpu.make_async_copy(cnt_vmem, cnt_smem, smem_sem)
        cnt_copy.start()
        cnt_copy.wait()

        def zcopy(e):
            start = pl.multiple_of((e * cap + cnt_smem[0, e]) * SUBLANES, SUBLANES)
            return pltpu.make_async_copy(zero_buf, xs_hbm.at[pl.ds(start, zero_buf.shape[0]), :], zero_sem)

        def zstart(e, c):
            zcopy(e).start()
            return c

        def zwait(e, c):
            zcopy(e).wait()
            return c

        lax.fori_loop(0, N_EXPERTS, zstart, 0)
        lax.fori_loop(0, N_EXPERTS, zwait, 0)


def _mixer_call(x, w_in, conv_w, lconv_w, lconv_b, w_gate, b_r, b_i, lam, w_out, ln_g, ln_b,
                w_rt, b_rt, alpha, expert_cap):
    bsz, seq, d = x.shape
    d_conv = conv_w.shape[1]
    d_lru = lconv_w.shape[1]
    ts = SEQ_TILE
    n_tok = bsz * seq
    nt = seq // ts

    def const(shape):
        return pl.BlockSpec(shape, lambda b, t: (0,) * len(shape))

    tok_map = lambda b, t: (b * nt + t, 0)
    kernel = functools.partial(_mixer_kernel, alpha, d_conv, d_lru)
    return pl.pallas_call(
        kernel,
        grid=(bsz, nt),
        in_specs=[
            pl.BlockSpec((None, ts, d), lambda b, t: (b, t, 0)),
            const(w_in.shape), const(conv_w.shape), const(lconv_w.shape), const(lconv_b.shape),
            const(w_gate.shape), const(b_r.shape), const(b_i.shape), const(lam.shape),
            const(w_out.shape), const(ln_g.shape), const(ln_b.shape), const(w_rt.shape),
            const(b_rt.shape),
        ],
        out_specs=[
            pl.BlockSpec((2 * TOP_K, ts), lambda b, t: (0, b * nt + t)),
            pl.BlockSpec((ts, LANES), tok_map),
            pl.BlockSpec((1, LANES), lambda b, t: (0, 0)),
            pl.BlockSpec(memory_space=pl.ANY),
            pl.BlockSpec(memory_space=pl.ANY),
        ],
        out_shape=[
            jax.ShapeDtypeStruct((2 * TOP_K, n_tok), jnp.int32),
            jax.ShapeDtypeStruct((n_tok, LANES), jnp.float32),
            jax.ShapeDtypeStruct((1, LANES), jnp.int32),
            jax.ShapeDtypeStruct((n_tok * SUBLANES, LANES), jnp.float32),
            jax.ShapeDtypeStruct((N_EXPERTS * expert_cap * SUBLANES, LANES), jnp.float32),
        ],
        scratch_shapes=[
            pltpu.VMEM((ts + SUBLANES, d_conv), jnp.float32),
            pltpu.VMEM((ts + SUBLANES, d_lru), jnp.float32),
            pltpu.VMEM((1, d_lru), jnp.float32),
            pltpu.VMEM((1, LANES), jnp.float32),
            pltpu.VMEM((ts, ts), jnp.bfloat16),
            pltpu.VMEM((ts * d_lru // LANES, LANES), jnp.float32),
            pltpu.VMEM((ts * d_lru // LANES, LANES), jnp.float32),
            pltpu.VMEM((ts * d_lru // LANES, LANES), jnp.float32),
            pltpu.VMEM((2, ts * SUBLANES, LANES), jnp.float32),
            pltpu.VMEM((SUBLANES, ts), jnp.int32),
            pltpu.SMEM((2 * SUBLANES, ts), jnp.int32),
            pltpu.VMEM((SUBLANES, LANES), jnp.int32),
            pltpu.SMEM((SUBLANES, LANES), jnp.int32),
            pltpu.VMEM((ROW_BLOCK * SUBLANES, LANES), jnp.float32),
            pltpu.SemaphoreType.DMA((2,)),
            pltpu.SemaphoreType.DMA((2,)),
            pltpu.SemaphoreType.DMA,
            pltpu.SemaphoreType.DMA,
        ],
        compiler_params=pltpu.CompilerParams(
            dimension_semantics=("arbitrary", "arbitrary"),
            vmem_limit_bytes=VMEM_LIMIT_BYTES),
        name="mixer_ln1_router",
    )(x, w_in, conv_w, lconv_w, lconv_b, w_gate, b_r, b_i, lam, w_out, ln_g, ln_b, w_rt, b_rt)


def _expert_kernel(d_expert, plan_ref, xs_ref, w_gu_hbm, b_gu_ref, w_dn_hbm, b_dn_ref, ys_ref,
                   wgu_buf, wdn_buf, sem):
    m = pl.program_id(0)
    bm = xs_ref.shape[0] // SUBLANES

    def fetch(e, s):
        return (pltpu.make_async_copy(w_gu_hbm.at[e], wgu_buf.at[s], sem.at[0, s]),
                pltpu.make_async_copy(w_dn_hbm.at[e], wdn_buf.at[s], sem.at[1, s]))

    @pl.when(m < plan_ref[4, 0])
    def _():
        e = plan_ref[0, m]
        s = plan_ref[3, m]

        @pl.when(m == 0)
        def _():
            for c in fetch(e, s):
                c.start()

        @pl.when(plan_ref[1, m] == 1)
        def _():
            nxt = plan_ref[2, m]

            @pl.when(nxt >= 0)
            def _():
                for c in fetch(nxt, 1 - s):
                    c.start()

            for c in fetch(e, s):
                c.wait()

        def ffn(n_rows):
            xb = _load_rows(xs_ref, 0, n_rows)
            h = jnp.dot(xb, wgu_buf[s], preferred_element_type=jnp.float32) + b_gu_ref[pl.ds(e, 1), :]
            h_glu = jnp.minimum(h[:, :d_expert], SWIGLU_LIMIT)
            h_lin = jnp.clip(h[:, d_expert:], -SWIGLU_LIMIT, SWIGLU_LIMIT)
            act = h_glu * jax.nn.sigmoid(SWIGLU_ALPHA * h_glu) * (h_lin + 1.0)
            y = jnp.dot(act, wdn_buf[s], preferred_element_type=jnp.float32) + b_dn_ref[pl.ds(e, 1), :]
            _store_rows(ys_ref, y)

        half_only = plan_ref[6, m] == 1

        @pl.when(jnp.logical_not(half_only))
        def _():
            ffn(bm)

        @pl.when(half_only)
        def _():
            ffn(bm // 2)
            rest = (bm - bm // 2) * SUBLANES
            ys_ref[pl.ds(bm // 2 * SUBLANES, rest), :] = jnp.zeros((rest, LANES), jnp.float32)


def _expert_call(plan, xs, w_gu, b_gu, w_dn, b_dn, n_blocks):
    n_exp, d, two_de = w_gu.shape
    d_expert = w_dn.shape[1]
    bm = ROW_BLOCK

    def row_map(m, plan_ref):
        return (jnp.minimum(m, plan_ref[4, 0] - 1), 0)

    def xs_map(m, plan_ref):
        return (plan_ref[5, jnp.minimum(m, plan_ref[4, 0] - 1)], 0)

    grid_spec = pltpu.PrefetchScalarGridSpec(
        num_scalar_prefetch=1,
        grid=(n_blocks,),
        in_specs=[
            pl.BlockSpec((bm * SUBLANES, LANES), xs_map),
            pl.BlockSpec(memory_space=pl.ANY),
            pl.BlockSpec((n_exp, two_de), lambda m, plan_ref: (0, 0)),
            pl.BlockSpec(memory_space=pl.ANY),
            pl.BlockSpec((n_exp, d), lambda m, plan_ref: (0, 0)),
        ],
        out_specs=pl.BlockSpec((bm * SUBLANES, LANES), row_map),
        scratch_shapes=[
            pltpu.VMEM((2, d, two_de), jnp.float32),
            pltpu.VMEM((2, d_expert, d), jnp.float32),
            pltpu.SemaphoreType.DMA((2, 2)),
        ],
    )
    return pl.pallas_call(
        functools.partial(_expert_kernel, d_expert),
        grid_spec=grid_spec,
        out_shape=jax.ShapeDtypeStruct((n_blocks * bm * SUBLANES, LANES), jnp.float32),
        compiler_params=pltpu.CompilerParams(
            dimension_semantics=("arbitrary",),
            vmem_limit_bytes=VMEM_LIMIT_BYTES),
        name="expert_ffn",
    )(plan, xs, w_gu, b_gu, w_dn, b_dn)


def _combine_kernel(alpha, slot8_ref, slot8_next_ref, ys_ref, x1t_ref, gates_ref, ln_g_ref, ln_b_ref,
                    out_ref, buf_a, buf_b, sem):
    step = pl.program_id(0)
    last = pl.num_programs(0) - 1
    tt = out_ref.shape[0] // 2

    def gather(slot_ref, half, buf, s):
        def row_copy(j, k):
            return pltpu.make_async_copy(_row_tile(ys_ref, slot_ref[(half * tt + j) * TOP_K + k]),
                                         _row_tile(buf, (k * tt + j) * SUBLANES), sem.at[s])
        return row_copy

    def reduce_tile(half, buf):
        rows = pl.ds(half * tt, tt)
        gates = gates_ref[rows, :]
        z = alpha * _load_rows(x1t_ref, half * tt, tt)
        for k in range(TOP_K):
            z = z + gates[:, k:k + 1] * _load_rows(buf, k * tt, tt)
        out_ref[rows, :] = _layer_norm(z, ln_g_ref[...], ln_b_ref[...])

    first_a = gather(slot8_ref, 0, buf_a, 0)
    second_b = gather(slot8_ref, 1, buf_b, 1)
    next_a = gather(slot8_next_ref, 0, buf_a, 0)

    @pl.when(step == 0)
    def _():
        _start_rows(first_a, tt)

    _start_rows(second_b, tt)
    _wait_rows(first_a, tt)
    reduce_tile(0, buf_a)

    @pl.when(step < last)
    def _():
        _start_rows(next_a, tt)

    _wait_rows(second_b, tt)
    reduce_tile(1, buf_b)


def _combine_call(slot8, ys, x1t, gates, ln_g, ln_b, alpha):
    n_tok = gates.shape[0]
    d = ln_g.shape[1]
    tt = TOKEN_TILE
    n_steps = n_tok // (2 * tt)
    return pl.pallas_call(
        functools.partial(_combine_kernel, alpha),
        grid=(n_steps,),
        in_specs=[
            pl.BlockSpec((2 * tt * TOP_K,), lambda i: (i,), memory_space=pltpu.SMEM),
            pl.BlockSpec((2 * tt * TOP_K,), lambda i: (jnp.minimum(i + 1, n_steps - 1),),
                         memory_space=pltpu.SMEM),
            pl.BlockSpec(memory_space=pl.ANY),
            pl.BlockSpec((2 * tt * SUBLANES, LANES), lambda i: (i, 0)),
            pl.BlockSpec((2 * tt, LANES), lambda i: (i, 0)),
            pl.BlockSpec((1, d), lambda i: (0, 0)),
            pl.BlockSpec((1, d), lambda i: (0, 0)),
        ],
        out_specs=pl.BlockSpec((2 * tt, d), lambda i: (i, 0)),
        out_shape=jax.ShapeDtypeStruct((n_tok, d), jnp.float32),
        scratch_shapes=[
            pltpu.VMEM((TOP_K * tt * SUBLANES, LANES), jnp.float32),
            pltpu.VMEM((TOP_K * tt * SUBLANES, LANES), jnp.float32),
            pltpu.SemaphoreType.DMA((2,)),
        ],
        compiler_params=pltpu.CompilerParams(
            dimension_semantics=("arbitrary",),
            vmem_limit_bytes=VMEM_LIMIT_BYTES),
        name="combine_ln2",
    )(slot8, slot8, ys, x1t, gates, ln_g, ln_b)


def _block_diag_gates(w_r, w_i):
    n_heads, dh, _ = w_r.shape
    per = GATE_GROUP // dh
    groups = []
    for gi in range(n_heads // per):
        def bd(w):
            m = jnp.zeros((GATE_GROUP, GATE_GROUP), w.dtype)
            for j in range(per):
                m = lax.dynamic_update_slice(m, w[gi * per + j], (j * dh, j * dh))
            return m
        groups.append(jnp.concatenate([bd(w_r), bd(w_i)], axis=1))
    return jnp.stack(groups)


def _layer(x, w_in, conv_w, lconv_w, lconv_b, w_r, b_r, w_i, b_i, lam, w_out, ln1_g, ln1_b,
           w_rt, b_rt, w_gu, b_gu, w_dn, b_dn, ln2_g, ln2_b, alpha):
    bsz, seq, d = x.shape
    assert d == SUBLANES * LANES, "tile-per-row layout needs D == 1024"
    n_tok = bsz * seq
    n_assign = n_tok * TOP_K
    n_blocks = -(-n_assign // ROW_BLOCK) + N_EXPERTS
    expert_cap = n_tok + ROW_BLOCK
    assert expert_cap % ROW_BLOCK == 0
    cap_blocks = expert_cap // ROW_BLOCK
    bf16 = jnp.bfloat16
    row = lambda v: v.reshape(1, -1)

    w_rt_p = jnp.zeros((d, LANES), jnp.float32).at[:, :N_EXPERTS].set(w_rt)
    w_rt_hi = w_rt_p.astype(bf16)
    w_rt_lo = (w_rt_p - w_rt_hi.astype(jnp.float32)).astype(bf16)
    w_rt_p = jnp.concatenate([w_rt_hi, w_rt_lo], axis=1)
    b_rt_p = jnp.zeros((1, LANES), jnp.float32).at[0, :N_EXPERTS].set(b_rt)
    route, gates, counts, x1t, xs = _mixer_call(
        x, w_in.astype(bf16), conv_w, lconv_w, row(lconv_b), _block_diag_gates(w_r, w_i).astype(bf16),
        row(b_r), row(b_i), row(lam), w_out.astype(bf16), row(ln1_g), row(ln1_b), w_rt_p, b_rt_p, alpha,
        expert_cap)

    experts = jnp.arange(N_EXPERTS, dtype=jnp.int32)
    counts = counts[0, :N_EXPERTS]
    padded = (counts + ROW_BLOCK - 1) // ROW_BLOCK * ROW_BLOCK
    pad_end = jnp.cumsum(padded)
    pad_start = pad_end - padded
    top_e = route[:TOP_K, :]
    rank = route[TOP_K:2 * TOP_K, :]
    onehot = top_e[:, :, None] == experts
    slot8 = ((jnp.sum(jnp.where(onehot, pad_start, 0), axis=-1) + rank) * SUBLANES).astype(jnp.int32)
    slot8 = slot8.T.reshape(-1)
    block_row0 = jnp.arange(n_blocks, dtype=jnp.int32) * ROW_BLOCK
    block_e = jnp.minimum(
        jnp.sum((pad_end[None, :] <= block_row0[:, None]).astype(jnp.int32), axis=1), N_EXPERTS - 1)
    n_used = (pad_end[-1:] // ROW_BLOCK).astype(jnp.int32)

    nonempty = counts > 0
    later_nonempty = jnp.where(nonempty[None, :] & (experts[None, :] > experts[:, None]),
                               experts[None, :], N_EXPERTS)
    next_tbl = jnp.min(later_nonempty, axis=1)
    next_tbl = jnp.where(next_tbl == N_EXPERTS, -1, next_tbl)
    buf_tbl = (jnp.cumsum(nonempty.astype(jnp.int32)) - 1) % 2
    is_e = block_e[:, None] == experts[None, :]
    look = lambda tbl: jnp.sum(jnp.where(is_e, tbl[None, :], 0), axis=1)
    first = (block_row0 == look(pad_start)).astype(jnp.int32)
    last_row = jnp.zeros((n_blocks,), jnp.int32).at[0].set(n_used[0])
    xs_block = block_e * cap_blocks + (block_row0 - look(pad_start)) // ROW_BLOCK
    rem = counts % ROW_BLOCK
    half_tbl = ((rem > 0) & (rem <= ROW_BLOCK // 2)).astype(jnp.int32)
    half_only = (block_row0 + ROW_BLOCK == look(pad_end)).astype(jnp.int32) * look(half_tbl)
    plan = jnp.stack([block_e, first, look(next_tbl), look(buf_tbl), last_row, xs_block,
                      half_only]).astype(jnp.int32)

    ys = _expert_call(plan, xs, w_gu, b_gu, w_dn, b_dn, n_blocks)
    out = _combine_call(slot8, ys, x1t, gates, row(ln2_g), row(ln2_b), alpha)
    return out.reshape(bsz, seq, d)


def kernel(x, w_in, conv_w, lru_conv_w, lru_conv_b, w_rgate, b_rgate, w_igate, b_igate, lru_lambda,
           w_out, ln1_g, ln1_b, w_router, b_router, w_gate_up, b_gate_up, w_down, b_down, ln2_g, ln2_b):
    depth = w_in.shape[0]
    alpha = (2.0 * depth) ** 0.25
    for l in range(depth):
        x = _layer(x, w_in[l], conv_w[l], lru_conv_w[l], lru_conv_b[l], w_rgate[l], b_rgate[l],
                   w_igate[l], b_igate[l], lru_lambda[l], w_out[l], ln1_g[l], ln1_b[l],
                   w_router[l], b_router[l], w_gate_up[l], b_gate_up[l], w_down[l], b_down[l],
                   ln2_g[l], ln2_b[l], alpha)
    return x
```

```python
import functools

import jax
import jax.numpy as jnp
from jax import lax
from jax.experimental import pallas as pl
from jax.experimental.pallas import tpu as pltpu

SHORT_CONV_WIDTH = 3
LRU_CONV_WIDTH = 4
LRU_C = 8.0
N_EXPERTS = 32
TOP_K = 4
SWIGLU_LIMIT = 7.0
SWIGLU_ALPHA = 1.702
LN_EPS = 1e-5

LANES = 128
SUBLANES = 8
MXU_DIM = 256
VMEM_LIMIT_BYTES = 56 * 1024 * 1024

SEQ_TILE = 512
TOKEN_TILE = 256
ROW_BLOCK = 512
GATE_GROUP = MXU_DIM
ISSUE_UNROLL = 8


def _layer_norm(z, g, b):
    mu = jnp.mean(z, axis=-1, keepdims=True)
    zc = z - mu
    var = jnp.mean(zc * zc, axis=-1, keepdims=True)
    return zc * lax.rsqrt(var + LN_EPS) * g + b


def _load_rows(ref, base, n):
    return jnp.concatenate(
        [ref[pl.ds(base * SUBLANES + s, n, stride=SUBLANES), :] for s in range(SUBLANES)], axis=-1)


def _store_rows(ref, val):
    n = val.shape[0]
    for s in range(SUBLANES):
        ref[pl.ds(s, n, stride=SUBLANES), :] = val[:, s * LANES:(s + 1) * LANES]


def _row_tile(ref, row8):
    return ref.at[pl.ds(pl.multiple_of(row8, SUBLANES), SUBLANES), :]


def _start_rows(row_copy, n_tok):
    def body(g, c):
        for jj in range(ISSUE_UNROLL):
            for k in range(TOP_K):
                row_copy(g * ISSUE_UNROLL + jj, k).start(priority=k % 2)
        return c
    lax.fori_loop(0, n_tok // ISSUE_UNROLL, body, 0)


def _wait_rows(row_copy, n_tok):
    def body(g, c):
        for jj in range(ISSUE_UNROLL):
            for k in range(TOP_K):
                row_copy(g * ISSUE_UNROLL + jj, k).wait()
        return c
    lax.fori_loop(0, n_tok // ISSUE_UNROLL, body, 0)


def _mixer_kernel(alpha, d_conv, d_lru,
                  x_ref, w_in_ref, conv_w_ref, lconv_w_ref, lconv_b_ref, w_gate_ref,
                  b_r_ref, b_i_ref, lam_ref, w_out_ref, ln_g_ref, ln_b_ref, w_rt_ref, b_rt_ref,
                  route_ref, gates_ref, counts_ref, x1t_hbm, xs_hbm,
                  g_buf, rx_buf, h_carry, cnt_carry, tri_buf, a_buf, b_buf, hs_buf,
                  stage, slot_vmem, slot_smem, cnt_vmem, cnt_smem, zero_buf,
                  row_sem, x1_sem, smem_sem, zero_sem):
    ts = x_ref.shape[0]
    hdr = SUBLANES
    first_of_seq = pl.program_id(1) == 0
    first_step = jnp.logical_and(pl.program_id(0) == 0, first_of_seq)
    step = pl.program_id(0) * pl.num_programs(1) + pl.program_id(1)
    last_step = step == pl.num_programs(0) * pl.num_programs(1) - 1
    par = step % 2
    cap = xs_hbm.shape[0] // (SUBLANES * N_EXPERTS)

    @pl.when(first_of_seq)
    def _():
        g_buf[0:hdr, :] = jnp.zeros((hdr, d_conv), jnp.float32)
        rx_buf[0:hdr, :] = jnp.zeros((hdr, d_lru), jnp.float32)
        h_carry[...] = jnp.zeros_like(h_carry)

    @pl.when(first_step)
    def _():
        cnt_carry[...] = jnp.zeros_like(cnt_carry)
        r = lax.broadcasted_iota(jnp.int32, (ts, ts), 0)
        c = lax.broadcasted_iota(jnp.int32, (ts, ts), 1)
        tri_buf[...] = (c < r).astype(jnp.bfloat16)

    x = x_ref[...]
    xb = x.astype(jnp.bfloat16)

    def proj(lo, hi):
        return jnp.dot(xb, w_in_ref[:, lo:hi], preferred_element_type=jnp.float32)

    c_b = proj(0, d_conv)
    g = proj(d_conv, 2 * d_conv) * proj(2 * d_conv, 3 * d_conv)
    g_buf[hdr:hdr + ts, :] = g
    y_conv = conv_w_ref[SHORT_CONV_WIDTH - 1:SHORT_CONV_WIDTH, :] * g
    for k in range(SHORT_CONV_WIDTH - 1):
        back = SHORT_CONV_WIDTH - 1 - k
        y_conv = y_conv + conv_w_ref[k:k + 1, :] * g_buf[hdr - back:hdr - back + ts, :]
    y_conv = c_b * y_conv
    g_buf[0:hdr, :] = g_buf[ts:ts + hdr, :]

    off = 3 * d_conv
    r_x = proj(off, off + d_lru)
    rx_buf[hdr:hdr + ts, :] = r_x
    u = lconv_w_ref[LRU_CONV_WIDTH - 1:LRU_CONV_WIDTH, :] * r_x + lconv_b_ref[...]
    for k in range(LRU_CONV_WIDTH - 1):
        back = LRU_CONV_WIDTH - 1 - k
        u = u + lconv_w_ref[k:k + 1, :] * rx_buf[hdr - back:hdr - back + ts, :]
    rx_buf[0:hdr, :] = rx_buf[ts:ts + hdr, :]

    ub = u.astype(jnp.bfloat16)
    r_parts, i_parts = [], []
    for gi in range(d_lru // GATE_GROUP):
        ri = jnp.dot(ub[:, gi * GATE_GROUP:(gi + 1) * GATE_GROUP], w_gate_ref[gi],
                     preferred_element_type=jnp.float32)
        r_parts.append(ri[:, :GATE_GROUP])
        i_parts.append(ri[:, GATE_GROUP:])
    r = jax.nn.sigmoid(jnp.concatenate(r_parts, axis=-1) + b_r_ref[...])
    i = jax.nn.sigmoid(jnp.concatenate(i_parts, axis=-1) + b_i_ref[...])

    neg_lam = -lam_ref[...]
    softplus = jnp.maximum(neg_lam, 0.0) + jnp.log1p(jnp.exp(-jnp.abs(neg_lam)))
    log_a = (-LRU_C) * r * softplus
    a = jnp.exp(log_a)
    bb = jnp.sqrt(jnp.tanh(-log_a) * (a * a + 1.0)) * (i * u)

    n_grp = ts // SUBLANES
    n_col = d_lru // LANES
    for c in range(n_col):
        a_buf[c * ts:(c + 1) * ts, :] = a[:, c * LANES:(c + 1) * LANES]
        b_buf[c * ts:(c + 1) * ts, :] = bb[:, c * LANES:(c + 1) * LANES]

    def slab(buf, j):
        return jnp.concatenate(
            [buf[pl.ds(c * ts + j, n_grp, stride=SUBLANES), :] for c in range(n_col)], axis=-1)

    cum_a, cum_b = [slab(a_buf, 0)], [slab(b_buf, 0)]
    for j in range(1, SUBLANES):
        a_j, b_j = slab(a_buf, j), slab(b_buf, j)
        cum_b.append(a_j * cum_b[-1] + b_j)
        cum_a.append(a_j * cum_a[-1])

    grp_row = lax.broadcasted_iota(jnp.int32, (n_grp, d_lru), 0)
    grp_a, grp_b = cum_a[-1], cum_b[-1]
    d = 1
    while d < n_grp:
        keep = grp_row >= d
        a_sh = jnp.where(keep, pltpu.roll(grp_a, d, 0), 1.0)
        b_sh = jnp.where(keep, pltpu.roll(grp_b, d, 0), 0.0)
        grp_b = grp_a * b_sh + grp_b
        grp_a = grp_a * a_sh
        d *= 2
    grp_end = grp_a * h_carry[...] + grp_b
    grp_start = jnp.where(grp_row == 0, h_carry[...], pltpu.roll(grp_end, 1, 0))
    h_carry[...] = grp_end[n_grp - 1:n_grp, :]

    for j in range(SUBLANES):
        h_j = cum_a[j] * grp_start + cum_b[j]
        for c in range(n_col):
            hs_buf[pl.ds(c * ts + j, n_grp, stride=SUBLANES), :] = h_j[:, c * LANES:(c + 1) * LANES]
    h = jnp.concatenate([hs_buf[c * ts:(c + 1) * ts, :] for c in range(n_col)], axis=-1)

    r_g = proj(off + d_lru, off + 2 * d_lru)
    y_lru = h * jax.nn.gelu(r_g)

    mixed = (jnp.dot(y_conv.astype(jnp.bfloat16), w_out_ref[0:d_conv, :],
                     preferred_element_type=jnp.float32)
             + jnp.dot(y_lru.astype(jnp.bfloat16), w_out_ref[d_conv:d_conv + d_lru, :],
                       preferred_element_type=jnp.float32))
    x1 = _layer_norm(alpha * x + mixed, ln_g_ref[...], ln_b_ref[...])
    _store_rows(stage.at[par], x1)

    x1_hi = x1.astype(jnp.bfloat16)
    x1_lo = (x1 - x1_hi.astype(jnp.float32)).astype(jnp.bfloat16)
    hi_part = jnp.dot(x1_hi, w_rt_ref[...], preferred_element_type=jnp.float32)
    lo_part = jnp.dot(x1_lo, w_rt_ref[:, 0:LANES], preferred_element_type=jnp.float32)
    logits = hi_part[:, 0:LANES] + hi_part[:, LANES:2 * LANES] + lo_part + b_rt_ref[...]
    lane = lax.broadcasted_iota(jnp.int32, (ts, LANES), 1)
    lane_f = lane.astype(jnp.float32)
    neg_inf = jnp.float32(-jnp.inf)
    work = jnp.where(lane < N_EXPERTS, logits, neg_inf)
    sel = jnp.zeros((ts, LANES), jnp.bool_)
    top_v, top_e = [], []
    for _ in range(TOP_K):
        m = jnp.max(work, axis=-1, keepdims=True)
        e_f = jnp.min(jnp.where(work == m, lane_f, float(LANES)), axis=-1, keepdims=True)
        hit = lane_f == e_f
        sel = jnp.logical_or(sel, hit)
        work = jnp.where(hit, neg_inf, work)
        top_v.append(m)
        top_e.append(e_f)
    exps = [jnp.exp(v - top_v[0]) for v in top_v]
    denom = exps[0]
    for ex in exps[1:]:
        denom = denom + ex

    sel_f = sel.astype(jnp.float32)
    before = jnp.dot(tri_buf[...], sel_f.astype(jnp.bfloat16), preferred_element_type=jnp.float32)
    pos = before + cnt_carry[...]
    cnt_carry[...] = cnt_carry[...] + jnp.sum(sel_f, axis=0, keepdims=True)
    counts = cnt_carry[...].astype(jnp.int32)
    counts_ref[...] = counts

    route = jnp.zeros((ts, LANES), jnp.float32)
    gates = jnp.zeros((ts, LANES), jnp.float32)
    for k in range(TOP_K):
        rank = jnp.sum(jnp.where(lane_f == top_e[k], pos, 0.0), axis=-1, keepdims=True)
        route = jnp.where(lane == k, top_e[k], route)
        route = jnp.where(lane == TOP_K + k, rank, route)
        gates = jnp.where(lane == k, exps[k] / denom, gates)
    gates_ref[...] = gates
    route_t = jnp.transpose(route)[0:2 * TOP_K, :].astype(jnp.int32)
    route_ref[...] = route_t

    slots8 = (route_t[0:TOP_K, :] * cap + route_t[TOP_K:2 * TOP_K, :]) * SUBLANES
    slot_vmem[...] = jnp.concatenate([slots8, jnp.zeros_like(slots8)], axis=0)
    def rows_of(p):
        def row_copy(j, k):
            return pltpu.make_async_copy(_row_tile(stage.at[p], j * SUBLANES),
                                         _row_tile(xs_hbm, slot_smem[p * SUBLANES + k, j]), row_sem.at[p])
        return row_copy

    def x1_copy(p, s):
        rows = ts * SUBLANES
        return pltpu.make_async_copy(stage.at[p], x1t_hbm.at[pl.ds(pl.multiple_of(s * rows, rows), rows), :],
                                     x1_sem.at[p])

    def dispatch_from(p):
        to_smem = pltpu.make_async_copy(
            slot_vmem, slot_smem.at[pl.ds(p * SUBLANES, SUBLANES), :], smem_sem)
        to_smem.start()

        @pl.when(step > 0)
        def _():
            _wait_rows(rows_of(1 - p), ts)
            x1_copy(1 - p, step - 1).wait()

        to_smem.wait()
        _start_rows(rows_of(p), ts)
        x1_copy(p, step).start()

        @pl.when(last_step)
        def _():
            _wait_rows(rows_of(p), ts)
            x1_copy(p, step).wait()

    for p in range(2):
        pl.when(par == p)(functools.partial(dispatch_from, p))

    @pl.when(last_step)
    def _():
        zero_buf[...] = jnp.zeros_like(zero_buf)
        cnt_vmem[...] = jnp.broadcast_to(counts, cnt_vmem.shape)
        cnt_copy = pltpu.make_async_copy(cnt_vmem, cnt_smem, smem_sem)
        cnt_copy.start()
        cnt_copy.wait()

        def zcopy(e):
            start = pl.multiple_of((e * cap + cnt_smem[0, e]) * SUBLANES, SUBLANES)
            return pltpu.make_async_copy(zero_buf, xs_hbm.at[pl.ds(start, zero_buf.shape[0]), :], zero_sem)

        def zstart(e, c):
            zcopy(e).start()
            return c

        def zwait(e, c):
            zcopy(e).wait()
            return c

        lax.fori_loop(0, N_EXPERTS, zstart, 0)
        lax.fori_loop(0, N_EXPERTS, zwait, 0)


def _mixer_call(x, w_in, conv_w, lconv_w, lconv_b, w_gate, b_r, b_i, lam, w_out, ln_g, ln_b,
                w_rt, b_rt, alpha, expert_cap):
    bsz, seq, d = x.shape
    d_conv = conv_w.shape[1]
    d_lru = lconv_w.shape[1]
    ts = SEQ_TILE
    n_tok = bsz * seq
    nt = seq // ts

    def const(shape):
        return pl.BlockSpec(shape, lambda b, t: (0,) * len(shape))

    tok_map = lambda b, t: (b * nt + t, 0)
    kernel = functools.partial(_mixer_kernel, alpha, d_conv, d_lru)
    return pl.pallas_call(
        kernel,
        grid=(bsz, nt),
        in_specs=[
            pl.BlockSpec((None, ts, d), lambda b, t: (b, t, 0)),
            const(w_in.shape), const(conv_w.shape), const(lconv_w.shape), const(lconv_b.shape),
            const(w_gate.shape), const(b_r.shape), const(b_i.shape), const(lam.shape),
            const(w_out.shape), const(ln_g.shape), const(ln_b.shape), const(w_rt.shape),
            const(b_rt.shape),
        ],
        out_specs=[
            pl.BlockSpec((2 * TOP_K, ts), lambda b, t: (0, b * nt + t)),
            pl.BlockSpec((ts, LANES), tok_map),
            pl.BlockSpec((1, LANES), lambda b, t: (0, 0)),
            pl.BlockSpec(memory_space=pl.ANY),
            pl.BlockSpec(memory_space=pl.ANY),
        ],
        out_shape=[
            jax.ShapeDtypeStruct((2 * TOP_K, n_tok), jnp.int32),
            jax.ShapeDtypeStruct((n_tok, LANES), jnp.float32),
            jax.ShapeDtypeStruct((1, LANES), jnp.int32),
            jax.ShapeDtypeStruct((n_tok * SUBLANES, LANES), jnp.float32),
            jax.ShapeDtypeStruct((N_EXPERTS * expert_cap * SUBLANES, LANES), jnp.float32),
        ],
        scratch_shapes=[
            pltpu.VMEM((ts + SUBLANES, d_conv), jnp.float32),
            pltpu.VMEM((ts + SUBLANES, d_lru), jnp.float32),
            pltpu.VMEM((1, d_lru), jnp.float32),
            pltpu.VMEM((1, LANES), jnp.float32),
            pltpu.VMEM((ts, ts), jnp.bfloat16),
            pltpu.VMEM((ts * d_lru // LANES, LANES), jnp.float32),
            pltpu.VMEM((ts * d_lru // LANES, LANES), jnp.float32),
            pltpu.VMEM((ts * d_lru // LANES, LANES), jnp.float32),
            pltpu.VMEM((2, ts * SUBLANES, LANES), jnp.float32),
            pltpu.VMEM((SUBLANES, ts), jnp.int32),
            pltpu.SMEM((2 * SUBLANES, ts), jnp.int32),
            pltpu.VMEM((SUBLANES, LANES), jnp.int32),
            pltpu.SMEM((SUBLANES, LANES), jnp.int32),
            pltpu.VMEM((ROW_BLOCK * SUBLANES, LANES), jnp.float32),
            pltpu.SemaphoreType.DMA((2,)),
            pltpu.SemaphoreType.DMA((2,)),
            pltpu.SemaphoreType.DMA,
            pltpu.SemaphoreType.DMA,
        ],
        compiler_params=pltpu.CompilerParams(
            dimension_semantics=("arbitrary", "arbitrary"),
            vmem_limit_bytes=VMEM_LIMIT_BYTES),
        name="mixer_ln1_router",
    )(x, w_in, conv_w, lconv_w, lconv_b, w_gate, b_r, b_i, lam, w_out, ln_g, ln_b, w_rt, b_rt)


def _expert_kernel(d_expert, plan_ref, xs_ref, w_gu_hbm, b_gu_ref, w_dn_hbm, b_dn_ref, ys_ref,
                   wgu_buf, wdn_buf, sem):
    m = pl.program_id(0)
    bm = xs_ref.shape[0] // SUBLANES

    def fetch(e, s):
        return (pltpu.make_async_copy(w_gu_hbm.at[e], wgu_buf.at[s], sem.at[0, s]),
                pltpu.make_async_copy(w_dn_hbm.at[e], wdn_buf.at[s], sem.at[1, s]))

    @pl.when(m < plan_ref[4, 0])
    def _():
        e = plan_ref[0, m]
        s = plan_ref[3, m]

        @pl.when(m == 0)
        def _():
            for c in fetch(e, s):
                c.start()

        @pl.when(plan_ref[1, m] == 1)
        def _():
            nxt = plan_ref[2, m]

            @pl.when(nxt >= 0)
            def _():
                for c in fetch(nxt, 1 - s):
                    c.start()

            for c in fetch(e, s):
                c.wait()

        def ffn(n_rows):
            xb = _load_rows(xs_ref, 0, n_rows)
            h = jnp.dot(xb, wgu_buf[s], preferred_element_type=jnp.float32) + b_gu_ref[pl.ds(e, 1), :]
            h_glu = jnp.minimum(h[:, :d_expert], SWIGLU_LIMIT)
            h_lin = jnp.clip(h[:, d_expert:], -SWIGLU_LIMIT, SWIGLU_LIMIT)
            act = h_glu * jax.nn.sigmoid(SWIGLU_ALPHA * h_glu) * (h_lin + 1.0)
            y = jnp.dot(act, wdn_buf[s], preferred_element_type=jnp.float32) + b_dn_ref[pl.ds(e, 1), :]
            _store_rows(ys_ref, y)

        half_only = plan_ref[6, m] == 1

        @pl.when(jnp.logical_not(half_only))
        def _():
            ffn(bm)

        @pl.when(half_only)
        def _():
            ffn(bm // 2)
            rest = (bm - bm // 2) * SUBLANES
            ys_ref[pl.ds(bm // 2 * SUBLANES, rest), :] = jnp.zeros((rest, LANES), jnp.float32)


def _expert_call(plan, xs, w_gu, b_gu, w_dn, b_dn, n_blocks):
    n_exp, d, two_de = w_gu.shape
    d_expert = w_dn.shape[1]
    bm = ROW_BLOCK

    def row_map(m, plan_ref):
        return (jnp.minimum(m, plan_ref[4, 0] - 1), 0)

    def xs_map(m, plan_ref):
        return (plan_ref[5, jnp.minimum(m, plan_ref[4, 0] - 1)], 0)

    grid_spec = pltpu.PrefetchScalarGridSpec(
        num_scalar_prefetch=1,
        grid=(n_blocks,),
        in_specs=[
            pl.BlockSpec((bm * SUBLANES, LANES), xs_map),
            pl.BlockSpec(memory_space=pl.ANY),
            pl.BlockSpec((n_exp, two_de), lambda m, plan_ref: (0, 0)),
            pl.BlockSpec(memory_space=pl.ANY),
            pl.BlockSpec((n_exp, d), lambda m, plan_ref: (0, 0)),
        ],
        out_specs=pl.BlockSpec((bm * SUBLANES, LANES), row_map),
        scratch_shapes=[
            pltpu.VMEM((2, d, two_de), jnp.float32),
            pltpu.VMEM((2, d_expert, d), jnp.float32),
            pltpu.SemaphoreType.DMA((2, 2)),
        ],
    )
    return pl.pallas_call(
        functools.partial(_expert_kernel, d_expert),
        grid_spec=grid_spec,
        out_shape=jax.ShapeDtypeStruct((n_blocks * bm * SUBLANES, LANES), jnp.float32),
        compiler_params=pltpu.CompilerParams(
            dimension_semantics=("arbitrary",),
            vmem_limit_bytes=VMEM_LIMIT_BYTES),
        name="expert_ffn",
    )(plan, xs, w_gu, b_gu, w_dn, b_dn)


def _combine_kernel(alpha, slot8_ref, slot8_next_ref, ys_ref, x1t_ref, gates_ref, ln_g_ref, ln_b_ref,
                    out_ref, buf_a, buf_b, sem):
    step = pl.program_id(0)
    last = pl.num_programs(0) - 1
    tt = out_ref.shape[0] // 2

    def gather(slot_ref, half, buf, s):
        def row_copy(j, k):
            return pltpu.make_async_copy(_row_tile(ys_ref, slot_ref[(half * tt + j) * TOP_K + k]),
                                         _row_tile(buf, (k * tt + j) * SUBLANES), sem.at[s])
        return row_copy

    def reduce_tile(half, buf):
        rows = pl.ds(half * tt, tt)
        gates = gates_ref[rows, :]
        z = alpha * _load_rows(x1t_ref, half * tt, tt)
        for k in range(TOP_K):
            z = z + gates[:, k:k + 1] * _load_rows(buf, k * tt, tt)
        out_ref[rows, :] = _layer_norm(z, ln_g_ref[...], ln_b_ref[...])

    first_a = gather(slot8_ref, 0, buf_a, 0)
    second_b = gather(slot8_ref, 1, buf_b, 1)
    next_a = gather(slot8_next_ref, 0, buf_a, 0)

    @pl.when(step == 0)
    def _():
        _start_rows(first_a, tt)

    _start_rows(second_b, tt)
    _wait_rows(first_a, tt)
    reduce_tile(0, buf_a)

    @pl.when(step < last)
    def _():
        _start_rows(next_a, tt)

    _wait_rows(second_b, tt)
    reduce_tile(1, buf_b)


def _combine_call(slot8, ys, x1t, gates, ln_g, ln_b, alpha):
    n_tok = gates.shape[0]
    d = ln_g.shape[1]
    tt = TOKEN_TILE
    n_steps = n_tok // (2 * tt)
    return pl.pallas_call(
        functools.partial(_combine_kernel, alpha),
        grid=(n_steps,),
        in_specs=[
            pl.BlockSpec((2 * tt * TOP_K,), lambda i: (i,), memory_space=pltpu.SMEM),
            pl.BlockSpec((2 * tt * TOP_K,), lambda i: (jnp.minimum(i + 1, n_steps - 1),),
                         memory_space=pltpu.SMEM),
            pl.BlockSpec(memory_space=pl.ANY),
            pl.BlockSpec((2 * tt * SUBLANES, LANES), lambda i: (i, 0)),
            pl.BlockSpec((2 * tt, LANES), lambda i: (i, 0)),
            pl.BlockSpec((1, d), lambda i: (0, 0)),
            pl.BlockSpec((1, d), lambda i: (0, 0)),
        ],
        out_specs=pl.BlockSpec((2 * tt, d), lambda i: (i, 0)),
        out_shape=jax.ShapeDtypeStruct((n_tok, d), jnp.float32),
        scratch_shapes=[
            pltpu.VMEM((TOP_K * tt * SUBLANES, LANES), jnp.float32),
            pltpu.VMEM((TOP_K * tt * SUBLANES, LANES), jnp.float32),
            pltpu.SemaphoreType.DMA((2,)),
        ],
        compiler_params=pltpu.CompilerParams(
            dimension_semantics=("arbitrary",),
            vmem_limit_bytes=VMEM_LIMIT_BYTES),
        name="combine_ln2",
    )(slot8, slot8, ys, x1t, gates, ln_g, ln_b)


def _block_diag_gates(w_r, w_i):
    n_heads, dh, _ = w_r.shape
    per = GATE_GROUP // dh
    groups = []
    for gi in range(n_heads // per):
        def bd(w):
            m = jnp.zeros((GATE_GROUP, GATE_GROUP), w.dtype)
            for j in range(per):
                m = lax.dynamic_update_slice(m, w[gi * per + j], (j * dh, j * dh))
            return m
        groups.append(jnp.concatenate([bd(w_r), bd(w_i)], axis=1))
    return jnp.stack(groups)


def _layer(x, w_in, conv_w, lconv_w, lconv_b, w_r, b_r, w_i, b_i, lam, w_out, ln1_g, ln1_b,
           w_rt, b_rt, w_gu, b_gu, w_dn, b_dn, ln2_g, ln2_b, alpha):
    bsz, seq, d = x.shape
    assert d == SUBLANES * LANES, "tile-per-row layout needs D == 1024"
    n_tok = bsz * seq
    n_assign = n_tok * TOP_K
    n_blocks = -(-n_assign // ROW_BLOCK) + N_EXPERTS
    expert_cap = n_tok + ROW_BLOCK
    assert expert_cap % ROW_BLOCK == 0
    cap_blocks = expert_cap // ROW_BLOCK
    bf16 = jnp.bfloat16
    row = lambda v: v.reshape(1, -1)

    w_rt_p = jnp.zeros((d, LANES), jnp.float32).at[:, :N_EXPERTS].set(w_rt)
    w_rt_hi = w_rt_p.astype(bf16)
    w_rt_lo = (w_rt_p - w_rt_hi.astype(jnp.float32)).astype(bf16)
    w_rt_p = jnp.concatenate([w_rt_hi, w_rt_lo], axis=1)
    b_rt_p = jnp.zeros((1, LANES), jnp.float32).at[0, :N_EXPERTS].set(b_rt)
    route, gates, counts, x1t, xs = _mixer_call(
        x, w_in.astype(bf16), conv_w, lconv_w, row(lconv_b), _block_diag_gates(w_r, w_i).astype(bf16),
        row(b_r), row(b_i), row(lam), w_out.astype(bf16), row(ln1_g), row(ln1_b), w_rt_p, b_rt_p, alpha,
        expert_cap)

    experts = jnp.arange(N_EXPERTS, dtype=jnp.int32)
    counts = counts[0, :N_EXPERTS]
    padded = (counts + ROW_BLOCK - 1) // ROW_BLOCK * ROW_BLOCK
    pad_end = jnp.cumsum(padded)
    pad_start = pad_end - padded
    top_e = route[:TOP_K, :]
    rank = route[TOP_K:2 * TOP_K, :]
    onehot = top_e[:, :, None] == experts
    slot8 = ((jnp.sum(jnp.where(onehot, pad_start, 0), axis=-1) + rank) * SUBLANES).astype(jnp.int32)
    slot8 = slot8.T.reshape(-1)
    block_row0 = jnp.arange(n_blocks, dtype=jnp.int32) * ROW_BLOCK
    block_e = jnp.minimum(
        jnp.sum((pad_end[None, :] <= block_row0[:, None]).astype(jnp.int32), axis=1), N_EXPERTS - 1)
    n_used = (pad_end[-1:] // ROW_BLOCK).astype(jnp.int32)

    nonempty = counts > 0
    later_nonempty = jnp.where(nonempty[None, :] & (experts[None, :] > experts[:, None]),
                               experts[None, :], N_EXPERTS)
    next_tbl = jnp.min(later_nonempty, axis=1)
    next_tbl = jnp.where(next_tbl == N_EXPERTS, -1, next_tbl)
    buf_tbl = (jnp.cumsum(nonempty.astype(jnp.int32)) - 1) % 2
    is_e = block_e[:, None] == experts[None, :]
    look = lambda tbl: jnp.sum(jnp.where(is_e, tbl[None, :], 0), axis=1)
    first = (block_row0 == look(pad_start)).astype(jnp.int32)
    last_row = jnp.zeros((n_blocks,), jnp.int32).at[0].set(n_used[0])
    xs_block = block_e * cap_blocks + (block_row0 - look(pad_start)) // ROW_BLOCK
    rem = counts % ROW_BLOCK
    half_tbl = ((rem > 0) & (rem <= ROW_BLOCK // 2)).astype(jnp.int32)
    half_only = (block_row0 + ROW_BLOCK == look(pad_end)).astype(jnp.int32) * look(half_tbl)
    plan = jnp.stack([block_e, first, look(next_tbl), look(buf_tbl), last_row, xs_block,
                      half_only]).astype(jnp.int32)

    ys = _expert_call(plan, xs, w_gu, b_gu, w_dn, b_dn, n_blocks)
    out = _combine_call(slot8, ys, x1t, gates, row(ln2_g), row(ln2_b), alpha)
    return out.reshape(bsz, seq, d)


def kernel(x, w_in, conv_w, lru_conv_w, lru_conv_b, w_rgate, b_rgate, w_igate, b_igate, lru_lambda,
           w_out, ln1_g, ln1_b, w_router, b_router, w_gate_up, b_gate_up, w_down, b_down, ln2_g, ln2_b):
    depth = w_in.shape[0]
    alpha = (2.0 * depth) ** 0.25
    for l in range(depth):
        x = _layer(x, w_in[l], conv_w[l], lru_conv_w[l], lru_conv_b[l], w_rgate[l], b_rgate[l],
                   w_igate[l], b_igate[l], lru_lambda[l], w_out[l], ln1_g[l], ln1_b[l],
                   w_router[l], b_router[l], w_gate_up[l], b_gate_up[l], w_down[l], b_down[l],
                   ln2_g[l], ln2_b[l], alpha)
    return x
```

```python
import functools

import jax
import jax.numpy as jnp
from jax import lax
from jax.experimental import pallas as pl
from jax.experimental.pallas import tpu as pltpu

SHORT_CONV_WIDTH = 3
LRU_CONV_WIDTH = 4
LRU_C = 8.0
N_EXPERTS = 32
TOP_K = 4
SWIGLU_LIMIT = 7.0
SWIGLU_ALPHA = 1.702
LN_EPS = 1e-5

LANES = 128
SUBLANES = 8
MXU_DIM = 256
VMEM_LIMIT_BYTES = 56 * 1024 * 1024

SEQ_TILE = 512
TOKEN_TILE = 256
ROW_BLOCK = 512
GATE_GROUP = MXU_DIM
ISSUE_UNROLL = 8


def _layer_norm(z, g, b):
    mu = jnp.mean(z, axis=-1, keepdims=True)
    zc = z - mu
    var = jnp.mean(zc * zc, axis=-1, keepdims=True)
    return zc * lax.rsqrt(var + LN_EPS) * g + b


def _load_rows(ref, base, n):
    return jnp.concatenate(
        [ref[pl.ds(base * SUBLANES + s, n, stride=SUBLANES), :] for s in range(SUBLANES)], axis=-1)


def _store_rows(ref, val):
    n = val.shape[0]
    for s in range(SUBLANES):
        ref[pl.ds(s, n, stride=SUBLANES), :] = val[:, s * LANES:(s + 1) * LANES]


def _row_tile(ref, row8):
    return ref.at[pl.ds(pl.multiple_of(row8, SUBLANES), SUBLANES), :]


def _start_rows(row_copy, n_tok):
    def body(g, c):
        for jj in range(ISSUE_UNROLL):
            for k in range(TOP_K):
                row_copy(g * ISSUE_UNROLL + jj, k).start(priority=k % 2)
        return c
    lax.fori_loop(0, n_tok // ISSUE_UNROLL, body, 0)


def _wait_rows(row_copy, n_tok):
    def body(g, c):
        for jj in range(ISSUE_UNROLL):
            for k in range(TOP_K):
                row_copy(g * ISSUE_UNROLL + jj, k).wait()
        return c
    lax.fori_loop(0, n_tok // ISSUE_UNROLL, body, 0)


def _mixer_kernel(alpha, d_conv, d_lru,
                  x_ref, w_in_ref, conv_w_ref, lconv_w_ref, lconv_b_ref, w_gate_ref,
                  b_r_ref, b_i_ref, lam_ref, w_out_ref, ln_g_ref, ln_b_ref, w_rt_ref, b_rt_ref,
                  route_ref, gates_ref, counts_ref, x1t_hbm, xs_hbm,
                  g_buf, rx_buf, h_carry, cnt_carry, tri_buf, a_buf, b_buf, hs_buf,
                  stage, slot_vmem, slot_smem, cnt_vmem, cnt_smem, zero_buf,
                  row_sem, x1_sem, smem_sem, zero_sem):
    ts = x_ref.shape[0]
    hdr = SUBLANES
    first_of_seq = pl.program_id(1) == 0
    first_step = jnp.logical_and(pl.program_id(0) == 0, first_of_seq)
    step = pl.program_id(0) * pl.num_programs(1) + pl.program_id(1)
    last_step = step == pl.num_programs(0) * pl.num_programs(1) - 1
    par = step % 2
    cap = xs_hbm.shape[0] // (SUBLANES * N_EXPERTS)

    @pl.when(first_of_seq)
    def _():
        g_buf[0:hdr, :] = jnp.zeros((hdr, d_conv), jnp.float32)
        rx_buf[0:hdr, :] = jnp.zeros((hdr, d_lru), jnp.float32)
        h_carry[...] = jnp.zeros_like(h_carry)

    @pl.when(first_step)
    def _():
        cnt_carry[...] = jnp.zeros_like(cnt_carry)
        r = lax.broadcasted_iota(jnp.int32, (ts, ts), 0)
        c = lax.broadcasted_iota(jnp.int32, (ts, ts), 1)
        tri_buf[...] = (c < r).astype(jnp.bfloat16)

    x = x_ref[...]
    xb = x.astype(jnp.bfloat16)

    def proj(lo, hi):
        return jnp.dot(xb, w_in_ref[:, lo:hi], preferred_element_type=jnp.float32)

    c_b = proj(0, d_conv)
    g = proj(d_conv, 2 * d_conv) * proj(2 * d_conv, 3 * d_conv)
    g_buf[hdr:hdr + ts, :] = g
    y_conv = conv_w_ref[SHORT_CONV_WIDTH - 1:SHORT_CONV_WIDTH, :] * g
    for k in range(SHORT_CONV_WIDTH - 1):
        back = SHORT_CONV_WIDTH - 1 - k
        y_conv = y_conv + conv_w_ref[k:k + 1, :] * g_buf[hdr - back:hdr - back + ts, :]
    y_conv = c_b * y_conv
    g_buf[0:hdr, :] = g_buf[ts:ts + hdr, :]

    off = 3 * d_conv
    r_x = proj(off, off + d_lru)
    rx_buf[hdr:hdr + ts, :] = r_x
    u = lconv_w_ref[LRU_CONV_WIDTH - 1:LRU_CONV_WIDTH, :] * r_x + lconv_b_ref[...]
    for k in range(LRU_CONV_WIDTH - 1):
        back = LRU_CONV_WIDTH - 1 - k
        u = u + lconv_w_ref[k:k + 1, :] * rx_buf[hdr - back:hdr - back + ts, :]
    rx_buf[0:hdr, :] = rx_buf[ts:ts + hdr, :]

    ub = u.astype(jnp.bfloat16)
    r_parts, i_parts = [], []
    for gi in range(d_lru // GATE_GROUP):
        ri = jnp.dot(ub[:, gi * GATE_GROUP:(gi + 1) * GATE_GROUP], w_gate_ref[gi],
                     preferred_element_type=jnp.float32)
        r_parts.append(ri[:, :GATE_GROUP])
        i_parts.append(ri[:, GATE_GROUP:])
    r = jax.nn.sigmoid(jnp.concatenate(r_parts, axis=-1) + b_r_ref[...])
    i = jax.nn.sigmoid(jnp.concatenate(i_parts, axis=-1) + b_i_ref[...])

    neg_lam = -lam_ref[...]
    softplus = jnp.maximum(neg_lam, 0.0) + jnp.log1p(jnp.exp(-jnp.abs(neg_lam)))
    log_a = (-LRU_C) * r * softplus
    a = jnp.exp(log_a)
    bb = jnp.sqrt(jnp.tanh(-log_a) * (a * a + 1.0)) * (i * u)

    n_grp = ts // SUBLANES
    n_col = d_lru // LANES
    for c in range(n_col):
        a_buf[c * ts:(c + 1) * ts, :] = a[:, c * LANES:(c + 1) * LANES]
        b_buf[c * ts:(c + 1) * ts, :] = bb[:, c * LANES:(c + 1) * LANES]

    def slab(buf, j):
        return jnp.concatenate(
            [buf[pl.ds(c * ts + j, n_grp, stride=SUBLANES), :] for c in range(n_col)], axis=-1)

    cum_a, cum_b = [slab(a_buf, 0)], [slab(b_buf, 0)]
    for j in range(1, SUBLANES):
        a_j, b_j = slab(a_buf, j), slab(b_buf, j)
        cum_b.append(a_j * cum_b[-1] + b_j)
        cum_a.append(a_j * cum_a[-1])

    grp_row = lax.broadcasted_iota(jnp.int32, (n_grp, d_lru), 0)
    grp_a, grp_b = cum_a[-1], cum_b[-1]
    d = 1
    while d < n_grp:
        keep = grp_row >= d
        a_sh = jnp.where(keep, pltpu.roll(grp_a, d, 0), 1.0)
        b_sh = jnp.where(keep, pltpu.roll(grp_b, d, 0), 0.0)
        grp_b = grp_a * b_sh + grp_b
        grp_a = grp_a * a_sh
        d *= 2
    grp_end = grp_a * h_carry[...] + grp_b
    grp_start = jnp.where(grp_row == 0, h_carry[...], pltpu.roll(grp_end, 1, 0))
    h_carry[...] = grp_end[n_grp - 1:n_grp, :]

    for j in range(SUBLANES):
        h_j = cum_a[j] * grp_start + cum_b[j]
        for c in range(n_col):
            hs_buf[pl.ds(c * ts + j, n_grp, stride=SUBLANES), :] = h_j[:, c * LANES:(c + 1) * LANES]
    h = jnp.concatenate([hs_buf[c * ts:(c + 1) * ts, :] for c in range(n_col)], axis=-1)

    r_g = proj(off + d_lru, off + 2 * d_lru)
    y_lru = h * jax.nn.gelu(r_g)

    mixed = (jnp.dot(y_conv.astype(jnp.bfloat16), w_out_ref[0:d_conv, :],
                     preferred_element_type=jnp.float32)
             + jnp.dot(y_lru.astype(jnp.bfloat16), w_out_ref[d_conv:d_conv + d_lru, :],
                       preferred_element_type=jnp.float32))
    x1 = _layer_norm(alpha * x + mixed, ln_g_ref[...], ln_b_ref[...])
    _store_rows(stage.at[par], x1)

    x1_hi = x1.astype(jnp.bfloat16)
    x1_lo = (x1 - x1_hi.astype(jnp.float32)).astype(jnp.bfloat16)
    hi_part = jnp.dot(x1_hi, w_rt_ref[...], preferred_element_type=jnp.float32)
    lo_part = jnp.dot(x1_lo, w_rt_ref[:, 0:LANES], preferred_element_type=jnp.float32)
    logits = hi_part[:, 0:LANES] + hi_part[:, LANES:2 * LANES] + lo_part + b_rt_ref[...]
    lane = lax.broadcasted_iota(jnp.int32, (ts, LANES), 1)
    lane_f = lane.astype(jnp.float32)
    neg_inf = jnp.float32(-jnp.inf)
    work = jnp.where(lane < N_EXPERTS, logits, neg_inf)
    sel = jnp.zeros((ts, LANES), jnp.bool_)
    top_v, top_e = [], []
    for _ in range(TOP_K):
        m = jnp.max(work, axis=-1, keepdims=True)
        e_f = jnp.min(jnp.where(work == m, lane_f, float(LANES)), axis=-1, keepdims=True)
        hit = lane_f == e_f
        sel = jnp.logical_or(sel, hit)
        work = jnp.where(hit, neg_inf, work)
        top_v.append(m)
        top_e.append(e_f)
    exps = [jnp.exp(v - top_v[0]) for v in top_v]
    denom = exps[0]
    for ex in exps[1:]:
        denom = denom + ex

    sel_f = sel.astype(jnp.float32)
    before = jnp.dot(tri_buf[...], sel_f.astype(jnp.bfloat16), preferred_element_type=jnp.float32)
    pos = before + cnt_carry[...]
    cnt_carry[...] = cnt_carry[...] + jnp.sum(sel_f, axis=0, keepdims=True)
    counts = cnt_carry[...].astype(jnp.int32)
    counts_ref[...] = counts

    route = jnp.zeros((ts, LANES), jnp.float32)
    gates = jnp.zeros((ts, LANES), jnp.float32)
    for k in range(TOP_K):
        rank = jnp.sum(jnp.where(lane_f == top_e[k], pos, 0.0), axis=-1, keepdims=True)
        route = jnp.where(lane == k, top_e[k], route)
        route = jnp.where(lane == TOP_K + k, rank, route)
        gates = jnp.where(lane == k, exps[k] / denom, gates)
    gates_ref[...] = gates
    route_t = jnp.transpose(route)[0:2 * TOP_K, :].astype(jnp.int32)
    route_ref[...] = route_t

    slots8 = (route_t[0:TOP_K, :] * cap + route_t[TOP_K:2 * TOP_K, :]) * SUBLANES
    slot_vmem[...] = jnp.concatenate([slots8, jnp.zeros_like(slots8)], axis=0)
    def rows_of(p):
        def row_copy(j, k):
            return pltpu.make_async_copy(_row_tile(stage.at[p], j * SUBLANES),
                                         _row_tile(xs_hbm, slot_smem[p * SUBLANES + k, j]), row_sem.at[p])
        return row_copy

    def x1_copy(p, s):
        rows = ts * SUBLANES
        return pltpu.make_async_copy(stage.at[p], x1t_hbm.at[pl.ds(pl.multiple_of(s * rows, rows), rows), :],
                                     x1_sem.at[p])

    def dispatch_from(p):
        to_smem = pltpu.make_async_copy(
            slot_vmem, slot_smem.at[pl.ds(p * SUBLANES, SUBLANES), :], smem_sem)
        to_smem.start()

        @pl.when(step > 0)
        def _():
            _wait_rows(rows_of(1 - p), ts)
            x1_copy(1 - p, step - 1).wait()

        to_smem.wait()
        _start_rows(rows_of(p), ts)
        x1_copy(p, step).start()

        @pl.when(last_step)
        def _():
            _wait_rows(rows_of(p), ts)
            x1_copy(p, step).wait()

    for p in range(2):
        pl.when(par == p)(functools.partial(dispatch_from, p))

    @pl.when(last_step)
    def _():
        zero_buf[...] = jnp.zeros_like(zero_buf)
        cnt_vmem[...] = jnp.broadcast_to(counts, cnt_vmem.shape)
        cnt_copy = pltpu.make_async_copy(cnt_vmem, cnt_smem, smem_sem)
        cnt_copy.start()
        cnt_copy.wait()

        def zcopy(e):
            start = pl.multiple_of((e * cap + cnt_smem[0, e]) * SUBLANES, SUBLANES)
            return pltpu.make_async_copy(zero_buf, xs_hbm.at[pl.ds(start, zero_buf.shape[0]), :], zero_sem)

        def zstart(e, c):
            zcopy(e).start()
            return c

        def zwait(e, c):
            zcopy(e).wait()
            return c

        lax.fori_loop(0, N_EXPERTS, zstart, 0)
        lax.fori_loop(0, N_EXPERTS, zwait, 0)


def _mixer_call(x, w_in, conv_w, lconv_w, lconv_b, w_gate, b_r, b_i, lam, w_out, ln_g, ln_b,
                w_rt, b_rt, alpha, expert_cap):
    bsz, seq, d = x.shape
    d_conv = conv_w.shape[1]
    d_lru = lconv_w.shape[1]
    ts = SEQ_TILE
    n_tok = bsz * seq
    nt = seq // ts

    def const(shape):
        return pl.BlockSpec(shape, lambda b, t: (0,) * len(shape))

    tok_map = lambda b, t: (b * nt + t, 0)
    kernel = functools.partial(_mixer_kernel, alpha, d_conv, d_lru)
    return pl.pallas_call(
        kernel,
        grid=(bsz, nt),
        in_specs=[
            pl.BlockSpec((None, ts, d), lambda b, t: (b, t, 0)),
            const(w_in.shape), const(conv_w.shape), const(lconv_w.shape), const(lconv_b.shape),
            const(w_gate.shape), const(b_r.shape), const(b_i.shape), const(lam.shape),
            const(w_out.shape), const(ln_g.shape), const(ln_b.shape), const(w_rt.shape),
            const(b_rt.shape),
        ],
        out_specs=[
            pl.BlockSpec((2 * TOP_K, ts), lambda b, t: (0, b * nt + t)),
            pl.BlockSpec((ts, LANES), tok_map),
            pl.BlockSpec((1, LANES), lambda b, t: (0, 0)),
            pl.BlockSpec(memory_space=pl.ANY),
            pl.BlockSpec(memory_space=pl.ANY),
        ],
        out_shape=[
            jax.ShapeDtypeStruct((2 * TOP_K, n_tok), jnp.int32),
            jax.ShapeDtypeStruct((n_tok, LANES), jnp.float32),
            jax.ShapeDtypeStruct((1, LANES), jnp.int32),
            jax.ShapeDtypeStruct((n_tok * SUBLANES, LANES), jnp.float32),
            jax.ShapeDtypeStruct((N_EXPERTS * expert_cap * SUBLANES, LANES), jnp.float32),
        ],
        scratch_shapes=[
            pltpu.VMEM((ts + SUBLANES, d_conv), jnp.float32),
            pltpu.VMEM((ts + SUBLANES, d_lru), jnp.float32),
            pltpu.VMEM((1, d_lru), jnp.float32),
            pltpu.VMEM((1, LANES), jnp.float32),
            pltpu.VMEM((ts, ts), jnp.bfloat16),
            pltpu.VMEM((ts * d_lru // LANES, LANES), jnp.float32),
            pltpu.VMEM((ts * d_lru // LANES, LANES), jnp.float32),
            pltpu.VMEM((ts * d_lru // LANES, LANES), jnp.float32),
            pltpu.VMEM((2, ts * SUBLANES, LANES), jnp.float32),
            pltpu.VMEM((SUBLANES, ts), jnp.int32),
            pltpu.SMEM((2 * SUBLANES, ts), jnp.int32),
            pltpu.VMEM((SUBLANES, LANES), jnp.int32),
            pltpu.SMEM((SUBLANES, LANES), jnp.int32),
            pltpu.VMEM((ROW_BLOCK * SUBLANES, LANES), jnp.float32),
            pltpu.SemaphoreType.DMA((2,)),
            pltpu.SemaphoreType.DMA((2,)),
            pltpu.SemaphoreType.DMA,
            pltpu.SemaphoreType.DMA,
        ],
        compiler_params=pltpu.CompilerParams(
            dimension_semantics=("arbitrary", "arbitrary"),
            vmem_limit_bytes=VMEM_LIMIT_BYTES),
        name="mixer_ln1_router",
    )(x, w_in, conv_w, lconv_w, lconv_b, w_gate, b_r, b_i, lam, w_out, ln_g, ln_b, w_rt, b_rt)


def _expert_kernel(d_expert, plan_ref, xs_ref, w_gu_hbm, b_gu_ref, w_dn_hbm, b_dn_ref, ys_ref,
                   wgu_buf, wdn_buf, sem):
    m = pl.program_id(0)
    bm = xs_ref.shape[0] // SUBLANES

    def fetch(e, s):
        return (pltpu.make_async_copy(w_gu_hbm.at[e], wgu_buf.at[s], sem.at[0, s]),
                pltpu.make_async_copy(w_dn_hbm.at[e], wdn_buf.at[s], sem.at[1, s]))

    @pl.when(m < plan_ref[4, 0])
    def _():
        e = plan_ref[0, m]
        s = plan_ref[3, m]

        @pl.when(m == 0)
        def _():
            for c in fetch(e, s):
                c.start()

        @pl.when(plan_ref[1, m] == 1)
        def _():
            nxt = plan_ref[2, m]

            @pl.when(nxt >= 0)
            def _():
                for c in fetch(nxt, 1 - s):
                    c.start()

            for c in fetch(e, s):
                c.wait()

        def ffn(n_rows):
            bf16 = jnp.bfloat16
            xb = _load_rows(xs_ref, 0, n_rows).astype(bf16)
            h = (jnp.dot(xb, wgu_buf[s].astype(bf16), preferred_element_type=jnp.float32)
                 + b_gu_ref[pl.ds(e, 1), :])
            h_glu = jnp.minimum(h[:, :d_expert], SWIGLU_LIMIT)
            h_lin = jnp.clip(h[:, d_expert:], -SWIGLU_LIMIT, SWIGLU_LIMIT)
            act = (h_glu * jax.nn.sigmoid(SWIGLU_ALPHA * h_glu) * (h_lin + 1.0)).astype(bf16)
            y = (jnp.dot(act, wdn_buf[s].astype(bf16), preferred_element_type=jnp.float32)
                 + b_dn_ref[pl.ds(e, 1), :])
            _store_rows(ys_ref, y)

        half_only = plan_ref[6, m] == 1

        @pl.when(jnp.logical_not(half_only))
        def _():
            ffn(bm)

        @pl.when(half_only)
        def _():
            ffn(bm // 2)
            rest = (bm - bm // 2) * SUBLANES
            ys_ref[pl.ds(bm // 2 * SUBLANES, rest), :] = jnp.zeros((rest, LANES), jnp.float32)


def _expert_call(plan, xs, w_gu, b_gu, w_dn, b_dn, n_blocks):
    n_exp, d, two_de = w_gu.shape
    d_expert = w_dn.shape[1]
    bm = ROW_BLOCK

    def row_map(m, plan_ref):
        return (jnp.minimum(m, plan_ref[4, 0] - 1), 0)

    def xs_map(m, plan_ref):
        return (plan_ref[5, jnp.minimum(m, plan_ref[4, 0] - 1)], 0)

    grid_spec = pltpu.PrefetchScalarGridSpec(
        num_scalar_prefetch=1,
        grid=(n_blocks,),
        in_specs=[
            pl.BlockSpec((bm * SUBLANES, LANES), xs_map),
            pl.BlockSpec(memory_space=pl.ANY),
            pl.BlockSpec((n_exp, two_de), lambda m, plan_ref: (0, 0)),
            pl.BlockSpec(memory_space=pl.ANY),
            pl.BlockSpec((n_exp, d), lambda m, plan_ref: (0, 0)),
        ],
        out_specs=pl.BlockSpec((bm * SUBLANES, LANES), row_map),
        scratch_shapes=[
            pltpu.VMEM((2, d, two_de), jnp.float32),
            pltpu.VMEM((2, d_expert, d), jnp.float32),
            pltpu.SemaphoreType.DMA((2, 2)),
        ],
    )
    return pl.pallas_call(
        functools.partial(_expert_kernel, d_expert),
        grid_spec=grid_spec,
        out_shape=jax.ShapeDtypeStruct((n_blocks * bm * SUBLANES, LANES), jnp.float32),
        compiler_params=pltpu.CompilerParams(
            dimension_semantics=("arbitrary",),
            vmem_limit_bytes=VMEM_LIMIT_BYTES),
        name="expert_ffn",
    )(plan, xs, w_gu, b_gu, w_dn, b_dn)


def _combine_kernel(alpha, slot8_ref, slot8_next_ref, ys_ref, x1t_ref, gates_ref, ln_g_ref, ln_b_ref,
                    out_ref, buf_a, buf_b, sem):
    step = pl.program_id(0)
    last = pl.num_programs(0) - 1
    tt = out_ref.shape[0] // 2

    def gather(slot_ref, half, buf, s):
        def row_copy(j, k):
            return pltpu.make_async_copy(_row_tile(ys_ref, slot_ref[(half * tt + j) * TOP_K + k]),
                                         _row_tile(buf, (k * tt + j) * SUBLANES), sem.at[s])
        return row_copy

    def reduce_tile(half, buf):
        rows = pl.ds(half * tt, tt)
        gates = gates_ref[rows, :]
        z = alpha * _load_rows(x1t_ref, half * tt, tt)
        for k in range(TOP_K):
            z = z + gates[:, k:k + 1] * _load_rows(buf, k * tt, tt)
        out_ref[rows, :] = _layer_norm(z, ln_g_ref[...], ln_b_ref[...])

    first_a = gather(slot8_ref, 0, buf_a, 0)
    second_b = gather(slot8_ref, 1, buf_b, 1)
    next_a = gather(slot8_next_ref, 0, buf_a, 0)

    @pl.when(step == 0)
    def _():
        _start_rows(first_a, tt)

    _start_rows(second_b, tt)
    _wait_rows(first_a, tt)
    reduce_tile(0, buf_a)

    @pl.when(step < last)
    def _():
        _start_rows(next_a, tt)

    _wait_rows(second_b, tt)
    reduce_tile(1, buf_b)


def _combine_call(slot8, ys, x1t, gates, ln_g, ln_b, alpha):
    n_tok = gates.shape[0]
    d = ln_g.shape[1]
    tt = TOKEN_TILE
    n_steps = n_tok // (2 * tt)
    return pl.pallas_call(
        functools.partial(_combine_kernel, alpha),
        grid=(n_steps,),
        in_specs=[
            pl.BlockSpec((2 * tt * TOP_K,), lambda i: (i,), memory_space=pltpu.SMEM),
            pl.BlockSpec((2 * tt * TOP_K,), lambda i: (jnp.minimum(i + 1, n_steps - 1),),
                         memory_space=pltpu.SMEM),
            pl.BlockSpec(memory_space=pl.ANY),
            pl.BlockSpec((2 * tt * SUBLANES, LANES), lambda i: (i, 0)),
            pl.BlockSpec((2 * tt, LANES), lambda i: (i, 0)),
            pl.BlockSpec((1, d), lambda i: (0, 0)),
            pl.BlockSpec((1, d), lambda i: (0, 0)),
        ],
        out_specs=pl.BlockSpec((2 * tt, d), lambda i: (i, 0)),
        out_shape=jax.ShapeDtypeStruct((n_tok, d), jnp.float32),
        scratch_shapes=[
            pltpu.VMEM((TOP_K * tt * SUBLANES, LANES), jnp.float32),
            pltpu.VMEM((TOP_K * tt * SUBLANES, LANES), jnp.float32),
            pltpu.SemaphoreType.DMA((2,)),
        ],
        compiler_params=pltpu.CompilerParams(
            dimension_semantics=("arbitrary",),
            vmem_limit_bytes=VMEM_LIMIT_BYTES),
        name="combine_ln2",
    )(slot8, slot8, ys, x1t, gates, ln_g, ln_b)


def _block_diag_gates(w_r, w_i):
    n_heads, dh, _ = w_r.shape
    per = GATE_GROUP // dh
    groups = []
    for gi in range(n_heads // per):
        def bd(w):
            m = jnp.zeros((GATE_GROUP, GATE_GROUP), w.dtype)
            for j in range(per):
                m = lax.dynamic_update_slice(m, w[gi * per + j], (j * dh, j * dh))
            return m
        groups.append(jnp.concatenate([bd(w_r), bd(w_i)], axis=1))
    return jnp.stack(groups)


def _layer(x, w_in, conv_w, lconv_w, lconv_b, w_r, b_r, w_i, b_i, lam, w_out, ln1_g, ln1_b,
           w_rt, b_rt, w_gu, b_gu, w_dn, b_dn, ln2_g, ln2_b, alpha):
    bsz, seq, d = x.shape
    assert d == SUBLANES * LANES, "tile-per-row layout needs D == 1024"
    n_tok = bsz * seq
    n_assign = n_tok * TOP_K
    n_blocks = -(-n_assign // ROW_BLOCK) + N_EXPERTS
    expert_cap = n_tok + ROW_BLOCK
    assert expert_cap % ROW_BLOCK == 0
    cap_blocks = expert_cap // ROW_BLOCK
    bf16 = jnp.bfloat16
    row = lambda v: v.reshape(1, -1)

    w_rt_p = jnp.zeros((d, LANES), jnp.float32).at[:, :N_EXPERTS].set(w_rt)
    w_rt_hi = w_rt_p.astype(bf16)
    w_rt_lo = (w_rt_p - w_rt_hi.astype(jnp.float32)).astype(bf16)
    w_rt_p = jnp.concatenate([w_rt_hi, w_rt_lo], axis=1)
    b_rt_p = jnp.zeros((1, LANES), jnp.float32).at[0, :N_EXPERTS].set(b_rt)
    route, gates, counts, x1t, xs = _mixer_call(
        x, w_in.astype(bf16), conv_w, lconv_w, row(lconv_b), _block_diag_gates(w_r, w_i).astype(bf16),
        row(b_r), row(b_i), row(lam), w_out.astype(bf16), row(ln1_g), row(ln1_b), w_rt_p, b_rt_p, alpha,
        expert_cap)

    experts = jnp.arange(N_EXPERTS, dtype=jnp.int32)
    counts = counts[0, :N_EXPERTS]
    padded = (counts + ROW_BLOCK - 1) // ROW_BLOCK * ROW_BLOCK
    pad_end = jnp.cumsum(padded)
    pad_start = pad_end - padded
    top_e = route[:TOP_K, :]
    rank = route[TOP_K:2 * TOP_K, :]
    onehot = top_e[:, :, None] == experts
    slot8 = ((jnp.sum(jnp.where(onehot, pad_start, 0), axis=-1) + rank) * SUBLANES).astype(jnp.int32)
    slot8 = slot8.T.reshape(-1)
    block_row0 = jnp.arange(n_blocks, dtype=jnp.int32) * ROW_BLOCK
    block_e = jnp.minimum(
        jnp.sum((pad_end[None, :] <= block_row0[:, None]).astype(jnp.int32), axis=1), N_EXPERTS - 1)
    n_used = (pad_end[-1:] // ROW_BLOCK).astype(jnp.int32)

    nonempty = counts > 0
    later_nonempty = jnp.where(nonempty[None, :] & (experts[None, :] > experts[:, None]),
                               experts[None, :], N_EXPERTS)
    next_tbl = jnp.min(later_nonempty, axis=1)
    next_tbl = jnp.where(next_tbl == N_EXPERTS, -1, next_tbl)
    buf_tbl = (jnp.cumsum(nonempty.astype(jnp.int32)) - 1) % 2
    is_e = block_e[:, None] == experts[None, :]
    look = lambda tbl: jnp.sum(jnp.where(is_e, tbl[None, :], 0), axis=1)
    first = (block_row0 == look(pad_start)).astype(jnp.int32)
    last_row = jnp.zeros((n_blocks,), jnp.int32).at[0].set(n_used[0])
    xs_block = block_e * cap_blocks + (block_row0 - look(pad_start)) // ROW_BLOCK
    rem = counts % ROW_BLOCK
    half_tbl = ((rem > 0) & (rem <= ROW_BLOCK // 2)).astype(jnp.int32)
    half_only = (block_row0 + ROW_BLOCK == look(pad_end)).astype(jnp.int32) * look(half_tbl)
    plan = jnp.stack([block_e, first, look(next_tbl), look(buf_tbl), last_row, xs_block,
                      half_only]).astype(jnp.int32)

    ys = _expert_call(plan, xs, w_gu, b_gu, w_dn, b_dn, n_blocks)
    out = _combine_call(slot8, ys, x1t, gates, row(ln2_g), row(ln2_b), alpha)
    return out.reshape(bsz, seq, d)


def kernel(x, w_in, conv_w, lru_conv_w, lru_conv_b, w_rgate, b_rgate, w_igate, b_igate, lru_lambda,
           w_out, ln1_g, ln1_b, w_router, b_router, w_gate_up, b_gate_up, w_down, b_down, ln2_g, ln2_b):
    depth = w_in.shape[0]
    alpha = (2.0 * depth) ** 0.25
    for l in range(depth):
        x = _layer(x, w_in[l], conv_w[l], lru_conv_w[l], lru_conv_b[l], w_rgate[l], b_rgate[l],
                   w_igate[l], b_igate[l], lru_lambda[l], w_out[l], ln1_g[l], ln1_b[l],
                   w_router[l], b_router[l], w_gate_up[l], b_gate_up[l], w_down[l], b_down[l],
                   ln2_g[l], ln2_b[l], alpha)
    return x
```
